```python
import math
import jax
import jax.numpy as jnp
from jax import lax
import numpy as np

D_MODEL = 4096
BATCH = 1
SEQ = 16384
DEPTH = 4

MLA_HEADS = 8
MLA_Q_RANK = 1536
MLA_KV_RANK = 512
MLA_NOPE_DIM = 128
MLA_ROPE_DIM = 64
MLA_V_DIM = 128
MLA_QK_DIM = MLA_NOPE_DIM + MLA_ROPE_DIM
DIFF_HEADS = 8
DIFF_HEAD_DIM = 64
DIFF_V_DIM = 2 * DIFF_HEAD_DIM
MEM_TOKENS = 256
MEM_HEADS = 4
MEM_HEAD_DIM = 128
D_FF = -(-8 * D_MODEL // (3 * 256)) * 256
N_BRANCHES = 2
REL_BUCKETS = 32
REL_MAX_DISTANCE = 128
ROPE_BASE = 10000.0
Q_BLOCK = 128
NORM_EPS = 1e-6
DIFF_SUBLN_EPS = 1e-5
IN_SPLIT_SIZES = (
    MLA_Q_RANK,
    MLA_KV_RANK,
    MLA_ROPE_DIM,
    DIFF_HEADS * 2 * DIFF_HEAD_DIM,
    DIFF_HEADS * 2 * DIFF_HEAD_DIM,
    DIFF_HEADS * DIFF_V_DIM,
    D_MODEL,
    D_MODEL,
)
D_IN = sum(IN_SPLIT_SIZES)

kernel_name = "hybrid_mla_diffattn_gated_encoder"


def rms_norm(x, gain, eps=NORM_EPS):
    x32 = x.astype(jnp.float32)
    y = x32 * lax.rsqrt(jnp.mean(x32 * x32, axis=-1, keepdims=True) + eps)
    return (y * gain.astype(jnp.float32)).astype(x.dtype)


def rope(x, pos):
    half = x.shape[-1] // 2
    inv_freq = ROPE_BASE ** (-jnp.arange(half, dtype=jnp.float32) / half)
    ang = pos.astype(jnp.float32)[:, :, None, None] * inv_freq
    cos, sin = jnp.cos(ang), jnp.sin(ang)
    x32 = x.astype(jnp.float32)
    x1, x2 = x32[..., :half], x32[..., half:]
    return jnp.concatenate([x1 * cos - x2 * sin, x1 * sin + x2 * cos], axis=-1).astype(x.dtype)


def t5_bucket(rel):
    nb = REL_BUCKETS // 2
    max_exact = nb // 2
    ret = jnp.where(rel > 0, nb, 0)
    n = jnp.abs(rel)
    n_f = jnp.maximum(n, 1).astype(jnp.float32)
    large = max_exact + (jnp.log(n_f / max_exact) / math.log(REL_MAX_DISTANCE / max_exact)
                         * (nb - max_exact)).astype(jnp.int32)
    large = jnp.minimum(large, nb - 1)
    return ret + jnp.where(n < max_exact, n, large)


def sweep_query_blocks(block_fn, seq):
    starts = jnp.arange(seq // Q_BLOCK, dtype=jnp.int32) * Q_BLOCK
    out = lax.map(block_fn, starts)
    nb, b, qb, h, dv = out.shape
    return jnp.moveaxis(out, 0, 1).reshape(b, nb * qb, h, dv)


def mla_branch(c_q, c_kv, k_rope_in, pos, q_norm, w_uq, kv_norm, w_ukv):
    b, s, _ = c_q.shape
    q = (rms_norm(c_q, q_norm) @ w_uq).reshape(b, s, MLA_HEADS, MLA_QK_DIM)
    q = jnp.concatenate([q[..., :MLA_NOPE_DIM], rope(q[..., MLA_NOPE_DIM:], pos)], axis=-1)
    kv = (rms_norm(c_kv, kv_norm) @ w_ukv).reshape(b, s, MLA_HEADS, MLA_NOPE_DIM + MLA_V_DIM)
    k_nope, v = kv[..., :MLA_NOPE_DIM], kv[..., MLA_NOPE_DIM:]
    k_rope = rope(k_rope_in[:, :, None, :], pos)
    k = jnp.concatenate([k_nope, jnp.broadcast_to(k_rope, (b, s, MLA_HEADS, MLA_ROPE_DIM))], axis=-1)
    scale = MLA_QK_DIM ** -0.5

    def block(start):
        qb = lax.dynamic_slice_in_dim(q, start, Q_BLOCK, axis=1)
        logits = jnp.einsum('bqhd,bkhd->bhqk', qb, k).astype(jnp.float32) * scale
        p = jax.nn.softmax(logits, axis=-1).astype(v.dtype)
        return jnp.einsum('bhqk,bkhd->bqhd', p, v)

    o = sweep_query_blocks(block, s)
    return o.reshape(b, s, MLA_HEADS * MLA_V_DIM)


def diff_branch(q_in, k_in, v_in, pos, rel_bias, lam_vecs, subln, layer_idx):
    b, s, _ = q_in.shape
    q = q_in.reshape(b, s, DIFF_HEADS, 2, DIFF_HEAD_DIM)
    k = k_in.reshape(b, s, DIFF_HEADS, 2, DIFF_HEAD_DIM)
    v = v_in.reshape(b, s, DIFF_HEADS, DIFF_V_DIM)
    q1, q2 = q[..., 0, :], q[..., 1, :]
    k1, k2 = k[..., 0, :], k[..., 1, :]
    lam_init = 0.8 - 0.6 * math.exp(-0.3 * layer_idx)
    lv = lam_vecs.astype(jnp.float32)
    lam = jnp.exp(jnp.sum(lv[0] * lv[1])) - jnp.exp(jnp.sum(lv[2] * lv[3])) + lam_init
    scale = DIFF_HEAD_DIM ** -0.5

    def block(start):
        q1b = lax.dynamic_slice_in_dim(q1, start, Q_BLOCK, axis=1)
        q2b = lax.dynamic_slice_in_dim(q2, start, Q_BLOCK, axis=1)
        pos_q = lax.dynamic_slice_in_dim(pos, start, Q_BLOCK, axis=1)
        bucket = t5_bucket(pos[:, None, :] - pos_q[:, :, None])
        bias = jnp.moveaxis(rel_bias[bucket], -1, 1).astype(jnp.float32)
        s1 = jnp.einsum('bqhd,bkhd->bhqk', q1b, k1).astype(jnp.float32) * scale + bias
        s2 = jnp.einsum('bqhd,bkhd->bhqk', q2b, k2).astype(jnp.float32) * scale + bias
        a = (jax.nn.softmax(s1, axis=-1) - lam * jax.nn.softmax(s2, axis=-1)).astype(v.dtype)
        return jnp.einsum('bhqk,bkhd->bqhd', a, v)

    o = sweep_query_blocks(block, s)
    o = rms_norm(o, subln, eps=DIFF_SUBLN_EPS) * (1.0 - lam_init)
    return o.reshape(b, s, DIFF_HEADS * DIFF_V_DIM)


def memory_cross_attention(h, mem, mem_norm, w_q, w_kv, w_o):
    b, s, _ = h.shape
    m_len = mem.shape[1]
    q = (h @ w_q).reshape(b, s, MEM_HEADS, MEM_HEAD_DIM)
    kv = (rms_norm(mem, mem_norm) @ w_kv).reshape(b, m_len, 2, MEM_HEADS, MEM_HEAD_DIM)
    k, v = kv[:, :, 0], kv[:, :, 1]
    logits = jnp.einsum('bqhd,bmhd->bhqm', q, k).astype(jnp.float32) * (MEM_HEAD_DIM ** -0.5)
    p = jax.nn.softmax(logits, axis=-1).astype(v.dtype)
    o = jnp.einsum('bhqm,bmhd->bqhd', p, v).reshape(b, s, MEM_HEADS * MEM_HEAD_DIM)
    return o @ w_o


def setup_inputs(seed: int = 0) -> dict:
    key = jax.random.key(seed)
    ks = jax.random.split(key, 32)
    f32 = jnp.float32
    L = DEPTH

    def dense(k, shape, fan_in):
        return jax.random.normal(k, shape, f32) * (fan_in ** -0.5)

    def gain(k, shape):
        return 1.0 + 0.05 * jax.random.normal(k, shape, f32)

    x = jax.random.normal(ks[0], (BATCH, SEQ, D_MODEL), f32)
    mem = jax.random.normal(ks[1], (BATCH, MEM_TOKENS, D_MODEL), f32)
    positions = (jnp.arange(SEQ, dtype=jnp.int32)[None, :]
                 + jax.random.randint(ks[2], (BATCH, 1), 0, 1024, dtype=jnp.int32))
    return {
        "x": x,
        "mem": mem,
        "positions": positions,
        "rel_bias": 0.5 * jax.random.normal(ks[3], (REL_BUCKETS, DIFF_HEADS), f32),
        "mix_norm_pre": gain(ks[4], (L, D_MODEL)),
        "mix_norm_post": gain(ks[5], (L, D_MODEL)),
        "w_in": dense(ks[6], (L, D_MODEL, D_IN), D_MODEL),
        "mla_q_norm": gain(ks[7], (L, MLA_Q_RANK)),
        "mla_w_uq": dense(ks[8], (L, MLA_Q_RANK, MLA_HEADS * MLA_QK_DIM), MLA_Q_RANK),
        "mla_kv_norm": gain(ks[9], (L, MLA_KV_RANK)),
        "mla_w_ukv": dense(ks[10], (L, MLA_KV_RANK, MLA_HEADS * (MLA_NOPE_DIM + MLA_V_DIM)), MLA_KV_RANK),
        "diff_lambda": 0.1 * jax.random.normal(ks[11], (L, 4, DIFF_HEAD_DIM), f32),
        "diff_subln": gain(ks[12], (L, DIFF_V_DIM)),
        "w_mla_branch": dense(ks[13], (L, MLA_HEADS * MLA_V_DIM, D_MODEL), MLA_HEADS * MLA_V_DIM),
        "w_diff_branch": dense(ks[14], (L, DIFF_HEADS * DIFF_V_DIM, D_MODEL), DIFF_HEADS * DIFF_V_DIM),
        "w_out": dense(ks[15], (L, D_MODEL, D_MODEL), D_MODEL),
        "xa_norm_pre": gain(ks[16], (L, D_MODEL)),
        "xa_norm_post": gain(ks[17], (L, D_MODEL)),
        "xa_mem_norm": gain(ks[18], (L, D_MODEL)),
        "xa_w_q": dense(ks[19], (L, D_MODEL, MEM_HEADS * MEM_HEAD_DIM), D_MODEL),
        "xa_w_kv": dense(ks[20], (L, D_MODEL, 2 * MEM_HEADS * MEM_HEAD_DIM), D_MODEL),
        "xa_w_o": dense(ks[21], (L, MEM_HEADS * MEM_HEAD_DIM, D_MODEL), MEM_HEADS * MEM_HEAD_DIM),
        "ffn_norm_pre": gain(ks[22], (L, D_MODEL)),
        "ffn_norm_post": gain(ks[23], (L, D_MODEL)),
        "ffn_w_in": dense(ks[24], (L, D_MODEL, 2 * D_FF), D_MODEL),
        "ffn_w_out": dense(ks[25], (L, D_FF, D_MODEL), D_FF),
    }


def reference(x, mem, positions, rel_bias, mix_norm_pre, mix_norm_post, w_in, mla_q_norm, mla_w_uq,
              mla_kv_norm, mla_w_ukv, diff_lambda, diff_subln, w_mla_branch, w_diff_branch, w_out,
              xa_norm_pre, xa_norm_post, xa_mem_norm, xa_w_q, xa_w_kv, xa_w_o,
              ffn_norm_pre, ffn_norm_post, ffn_w_in, ffn_w_out):
    splits = np.cumsum(IN_SPLIT_SIZES)[:-1].tolist()
    for l in range(DEPTH):
        h = rms_norm(x, mix_norm_pre[l])
        c_q, c_kv, k_rope, q_d, k_d, v_d, g_mla, g_diff = jnp.split(h @ w_in[l], splits, axis=-1)
        y_mla = mla_branch(c_q, c_kv, k_rope, positions, mla_q_norm[l], mla_w_uq[l],
                           mla_kv_norm[l], mla_w_ukv[l]) @ w_mla_branch[l]
        y_diff = diff_branch(q_d, k_d, v_d, positions, rel_bias, diff_lambda[l], diff_subln[l],
                             l) @ w_diff_branch[l]
        merged = jax.nn.sigmoid(g_mla) * y_mla + jax.nn.sigmoid(g_diff) * y_diff
        x = x + rms_norm(merged @ w_out[l], mix_norm_post[l])
        h = rms_norm(x, xa_norm_pre[l])
        y = memory_cross_attention(h, mem, xa_mem_norm[l], xa_w_q[l], xa_w_kv[l], xa_w_o[l])
        x = x + rms_norm(y, xa_norm_post[l])
        h = rms_norm(x, ffn_norm_pre[l])
        gate, up = jnp.split(h @ ffn_w_in[l], 2, axis=-1)
        x = x + rms_norm((jax.nn.silu(gate) * up) @ ffn_w_out[l], ffn_norm_post[l])
    return x
```

```python
import functools
import math

import jax
import jax.numpy as jnp
from jax import lax
from jax.experimental import pallas as pl
from jax.experimental.pallas import tpu as pltpu

F32 = jnp.float32
BF16 = jnp.bfloat16

MLA_HEADS = 8
MLA_NOPE_DIM = 128
MLA_ROPE_DIM = 64
MLA_V_DIM = 128
MLA_QK_DIM = MLA_NOPE_DIM + MLA_ROPE_DIM
DIFF_HEADS = 8
DIFF_HEAD_DIM = 64
DIFF_V_DIM = 2 * DIFF_HEAD_DIM
MEM_HEADS = 4
MEM_HEAD_DIM = 128
REL_BUCKETS = 32
REL_MAX_DISTANCE = 128
ROPE_BASE = 10000.0
NORM_EPS = 1e-6
DIFF_SUBLN_EPS = 1e-5

T5_LOG_THRESHOLDS = (12, 16, 23, 32, 46, 64, 91)

LANE = 128
VMEM_LIMIT_BYTES = 56 * 1024 * 1024
NEG_BIG = -1e30


def _cparams(sem):
    return pltpu.CompilerParams(dimension_semantics=sem, vmem_limit_bytes=VMEM_LIMIT_BYTES)


def _pick(dim, target, align=LANE):
    if dim <= target:
        return dim
    t = (target // align) * align
    while t >= align:
        if dim % t == 0:
            return t
        t -= align
    return dim


def _sigmoid(x):
    return 1.0 / (1.0 + jnp.exp(-x))


def _rms(x, eps):
    return x * lax.rsqrt(jnp.mean(x * x, axis=-1, keepdims=True) + eps)


def _rmsnorm_body(x_ref, g_ref, o_ref, *, eps):
    x = x_ref[...].astype(F32)
    o_ref[...] = (_rms(x, eps) * g_ref[...]).astype(o_ref.dtype)


def rmsnorm(x, g, out_dtype, eps=NORM_EPS):
    m, d = x.shape
    bm = _pick(m, 256, 8)
    return pl.pallas_call(
        functools.partial(_rmsnorm_body, eps=eps),
        grid=(m // bm,),
        in_specs=[pl.BlockSpec((bm, d), lambda i: (i, 0)), pl.BlockSpec((1, d), lambda i: (0, 0))],
        out_specs=pl.BlockSpec((bm, d), lambda i: (i, 0)),
        out_shape=jax.ShapeDtypeStruct((m, d), out_dtype),
        compiler_params=_cparams(("parallel",)),
        name="rmsnorm",
    )(x, g.reshape(1, d))


def _add_rmsnorm_body(x_ref, y_ref, g_ref, o_ref, *, eps):
    y = y_ref[...].astype(F32)
    o_ref[...] = x_ref[...] + _rms(y, eps) * g_ref[...]


def add_rmsnorm(x, y, g, eps=NORM_EPS):
    m, d = x.shape
    bm = _pick(m, 256, 8)
    return pl.pallas_call(
        functools.partial(_add_rmsnorm_body, eps=eps),
        grid=(m // bm,),
        in_specs=[pl.BlockSpec((bm, d), lambda i: (i, 0)), pl.BlockSpec((bm, d), lambda i: (i, 0)),
                  pl.BlockSpec((1, d), lambda i: (0, 0))],
        out_specs=pl.BlockSpec((bm, d), lambda i: (i, 0)),
        out_shape=jax.ShapeDtypeStruct((m, d), F32),
        compiler_params=_cparams(("parallel",)),
        name="add_rmsnorm",
    )(x, y, g.reshape(1, d))


def _matmul_body(a_ref, w_ref, o_ref, *scratch, nk):
    part = jnp.dot(a_ref[...], w_ref[...], preferred_element_type=F32)
    if nk == 1:
        o_ref[...] = part.astype(o_ref.dtype)
        return
    (acc_ref,) = scratch
    k = pl.program_id(2)

    @pl.when(k == 0)
    def _():
        acc_ref[...] = part

    @pl.when(k > 0)
    def _():
        acc_ref[...] += part

    @pl.when(k == nk - 1)
    def _():
        o_ref[...] = acc_ref[...].astype(o_ref.dtype)


def matmul(a, w, out_dtype, bm=1024, bn=1024, bk=4096):
    m, kd = a.shape
    _, n = w.shape
    bm, bn, bk = _pick(m, bm, 8), _pick(n, bn), _pick(kd, bk)
    nk = kd // bk
    return pl.pallas_call(
        functools.partial(_matmul_body, nk=nk),
        grid=(m // bm, n // bn, nk),
        in_specs=[pl.BlockSpec((bm, bk), lambda i, j, k: (i, k)),
                  pl.BlockSpec((bk, bn), lambda i, j, k: (k, j))],
        out_specs=pl.BlockSpec((bm, bn), lambda i, j, k: (i, j)),
        out_shape=jax.ShapeDtypeStruct((m, n), out_dtype),
        scratch_shapes=[pltpu.VMEM((bm, bn), F32)] if nk > 1 else [],
        compiler_params=_cparams(("parallel", "parallel", "arbitrary")),
        name="matmul",
    )(a, w)


def _swiglu_body(a_ref, wg_ref, wu_ref, o_ref):
    a = a_ref[...]
    g = jnp.dot(a, wg_ref[...], preferred_element_type=F32)
    u = jnp.dot(a, wu_ref[...], preferred_element_type=F32)
    o_ref[...] = (g * _sigmoid(g) * u).astype(o_ref.dtype)


def swiglu_matmul(a, w_gu, dff, bm=1024, bn=512):
    m, kd = a.shape
    bm, bn = _pick(m, bm, 8), _pick(dff, bn)
    nj = dff // bn
    return pl.pallas_call(
        _swiglu_body,
        grid=(m // bm, nj),
        in_specs=[pl.BlockSpec((bm, kd), lambda i, j: (i, 0)),
                  pl.BlockSpec((kd, bn), lambda i, j: (0, j)),
                  pl.BlockSpec((kd, bn), lambda i, j: (0, j + nj))],
        out_specs=pl.BlockSpec((bm, bn), lambda i, j: (i, j)),
        out_shape=jax.ShapeDtypeStruct((m, dff), BF16),
        compiler_params=_cparams(("parallel", "parallel")),
        name="swiglu_matmul",
    )(a, w_gu, w_gu)


def _merge_body(om_ref, od_ref, wm_ref, wd_ref, gm_ref, gd_ref, o_ref):
    ym = jnp.dot(om_ref[...], wm_ref[...], preferred_element_type=F32)
    yd = jnp.dot(od_ref[...], wd_ref[...], preferred_element_type=F32)
    sm = _sigmoid(gm_ref[...].astype(F32))
    sd = _sigmoid(gd_ref[...].astype(F32))
    o_ref[...] = (sm * ym + sd * yd).astype(o_ref.dtype)


def gated_merge(o_mla, o_diff, w_mla, w_diff, gates, bm=1024, bn=1024):
    m, km = o_mla.shape
    _, kd = o_diff.shape
    _, n = w_mla.shape
    bm, bn = _pick(m, bm, 8), _pick(n, bn)
    nj = n // bn
    return pl.pallas_call(
        _merge_body,
        grid=(m // bm, nj),
        in_specs=[pl.BlockSpec((bm, km), lambda i, j: (i, 0)),
                  pl.BlockSpec((bm, kd), lambda i, j: (i, 0)),
                  pl.BlockSpec((km, bn), lambda i, j: (0, j)),
                  pl.BlockSpec((kd, bn), lambda i, j: (0, j)),
                  pl.BlockSpec((bm, bn), lambda i, j: (i, j)),
                  pl.BlockSpec((bm, bn), lambda i, j: (i, j + nj))],
        out_specs=pl.BlockSpec((bm, bn), lambda i, j: (i, j)),
        out_shape=jax.ShapeDtypeStruct((m, n), BF16),
        compiler_params=_cparams(("parallel", "parallel")),
        name="gated_merge",
    )(o_mla, o_diff, w_mla, w_diff, gates, gates)


def _mla_prep_body(lat_ref, gq_ref, gkv_ref, wq_ref, wkv_ref, cos_ref, sin_ref,
                   q_ref, k_ref, v_ref, *, q_rank, kv_rank, scale):
    cos = cos_ref[...]
    sin = sin_ref[...]
    cq = lat_ref[:, :q_rank].astype(F32)
    ckv = lat_ref[:, q_rank:q_rank + kv_rank].astype(F32)
    kr = lat_ref[:, q_rank + kv_rank:q_rank + kv_rank + LANE].astype(F32)
    kr_rot = lat_ref[:, q_rank + kv_rank + LANE:q_rank + kv_rank + 2 * LANE].astype(F32)
    k_rope = (kr * cos + kr_rot * sin).astype(BF16)

    cqn = (_rms(cq, NORM_EPS) * gq_ref[...]).astype(BF16)
    qa = jnp.dot(cqn, wq_ref[...], preferred_element_type=F32)
    ckvn = (_rms(ckv, NORM_EPS) * gkv_ref[...]).astype(BF16)
    kv = jnp.dot(ckvn, wkv_ref[...], preferred_element_type=F32)

    hw = MLA_NOPE_DIM + 2 * LANE
    for h in range(MLA_HEADS):
        base = h * hw
        nope = qa[:, base:base + MLA_NOPE_DIM]
        rp = qa[:, base + MLA_NOPE_DIM:base + MLA_NOPE_DIM + LANE]
        rr = qa[:, base + MLA_NOPE_DIM + LANE:base + hw]
        q_ref[:, h * 256:h * 256 + 128] = (nope * scale).astype(BF16)
        q_ref[:, h * 256 + 128:(h + 1) * 256] = ((rp * cos + rr * sin) * scale).astype(BF16)
        k_ref[:, h * 256:h * 256 + 128] = kv[:, h * 128:(h + 1) * 128].astype(BF16)
        k_ref[:, h * 256 + 128:(h + 1) * 256] = k_rope
    v_ref[...] = kv[:, MLA_HEADS * MLA_NOPE_DIM:].astype(BF16)


def mla_prep(lat, gq, gkv, wq, wkv, cos, sin):
    s, lw = lat.shape
    q_rank, kv_rank = gq.shape[0], gkv.shape[0]
    ts = _pick(s, 256, 8)
    hq = MLA_HEADS * 256
    hv = MLA_HEADS * MLA_V_DIM
    row = lambda i: (i, 0)
    fixed = lambda i: (0, 0)
    return pl.pallas_call(
        functools.partial(_mla_prep_body, q_rank=q_rank, kv_rank=kv_rank, scale=MLA_QK_DIM ** -0.5),
        grid=(s // ts,),
        in_specs=[pl.BlockSpec((ts, lw), row),
                  pl.BlockSpec((1, q_rank), fixed), pl.BlockSpec((1, kv_rank), fixed),
                  pl.BlockSpec(wq.shape, fixed), pl.BlockSpec(wkv.shape, fixed),
                  pl.BlockSpec((ts, LANE), row), pl.BlockSpec((ts, LANE), row)],
        out_specs=[pl.BlockSpec((ts, hq), row), pl.BlockSpec((ts, hq), row), pl.BlockSpec((ts, hv), row)],
        out_shape=[jax.ShapeDtypeStruct((s, hq), BF16), jax.ShapeDtypeStruct((s, hq), BF16),
                   jax.ShapeDtypeStruct((s, hv), BF16)],
        compiler_params=_cparams(("parallel",)),
        name="mla_prep",
    )(lat, gq.reshape(1, -1), gkv.reshape(1, -1), wq, wkv, cos, sin)


def _flash_update(s, m, l, acc_ref, j, vt):
    m_new = jnp.maximum(m, jnp.max(s, axis=0, keepdims=True))
    alpha = jnp.exp(m - m_new)
    p = jnp.exp(s - m_new)
    l_new = alpha * l + jnp.sum(p, axis=0, keepdims=True)
    acc_ref[j] = alpha * acc_ref[j] + jnp.dot(vt, p.astype(BF16), preferred_element_type=F32)
    return m_new, l_new


def _mla_attn_body(qT_ref, k_ref, vT_ref, o_ref, acc_ref, *, tk, nk):
    qT = qT_ref[...]
    tq = qT.shape[1]
    acc_ref[...] = jnp.zeros_like(acc_ref)

    def body(i, carry):
        m, l = carry
        off = pl.multiple_of(i * tk, tk)
        kt = k_ref[pl.ds(off, tk), :]
        vt = vT_ref[:, pl.ds(off, tk)]
        s = jnp.dot(kt, qT, preferred_element_type=F32)
        return _flash_update(s, m, l, acc_ref, 0, vt)

    init = (jnp.full((1, tq), NEG_BIG, F32), jnp.zeros((1, tq), F32))
    _, l = lax.fori_loop(0, nk, body, init)
    o_ref[...] = (acc_ref[0] / l).astype(o_ref.dtype)


def mla_attention(qT, k, vT, tq=256, tk=512):
    s = k.shape[0]
    tq, tk = _pick(s, tq), _pick(s, tk)
    nk = s // tk
    return pl.pallas_call(
        functools.partial(_mla_attn_body, tk=tk, nk=nk),
        grid=(MLA_HEADS, s // tq),
        in_specs=[pl.BlockSpec((256, tq), lambda h, i: (h, i)),
                  pl.BlockSpec((s, 256), lambda h, i: (0, h)),
                  pl.BlockSpec((MLA_V_DIM, s), lambda h, i: (h, 0))],
        out_specs=pl.BlockSpec((MLA_V_DIM, tq), lambda h, i: (h, i)),
        out_shape=jax.ShapeDtypeStruct((MLA_HEADS * MLA_V_DIM, s), BF16),
        scratch_shapes=[pltpu.VMEM((1, MLA_V_DIM, tq), F32)],
        compiler_params=_cparams(("parallel", "parallel")),
        name="mla_attention",
    )(qT, k, vT)


def _t5_bias(rel, tab_ref, h):
    n = jnp.abs(rel)
    large = jnp.full(rel.shape, REL_BUCKETS // 4, jnp.int32)
    for t in T5_LOG_THRESHOLDS:
        large = large + jnp.where(n >= t, 1, 0)
    bucket = jnp.where(n < REL_BUCKETS // 4, n, large) + jnp.where(rel > 0, REL_BUCKETS // 2, 0)
    level = [tab_ref[h, b] for b in range(REL_BUCKETS)]
    bit = 1
    while len(level) > 1:
        msk = (bucket & bit) != 0
        level = [jnp.where(msk, level[2 * i + 1], level[2 * i]) for i in range(len(level) // 2)]
        bit *= 2
    return level[0]


def _diff_attn_body(kmin_ref, kmax_ref, qmin_ref, qmax_ref,
                    tab_ref, lam_ref, qT_ref, k_ref, vT_ref, posq_ref, posk_ref, g_ref,
                    o_ref, acc_ref, *, tk, nk, out_scale):
    h = pl.program_id(0)
    qi = pl.program_id(1)
    qT = qT_ref[...]
    tq = qT.shape[1]
    row = lax.broadcasted_iota(jnp.int32, qT.shape, 0)
    zero = jnp.zeros_like(qT)
    q1 = jnp.where(row < DIFF_HEAD_DIM, qT, zero)
    q2 = jnp.where(row >= DIFF_HEAD_DIM, qT, zero)
    b_after = tab_ref[h, REL_BUCKETS - 1]
    b_before = tab_ref[h, REL_BUCKETS // 2 - 1]
    q_lo = qmin_ref[qi]
    q_hi = qmax_ref[qi]
    acc_ref[...] = jnp.zeros_like(acc_ref)

    def body(i, carry):
        m1, l1, m2, l2 = carry
        off = pl.multiple_of(i * tk, tk)
        kt = k_ref[pl.ds(off, tk), :]
        vt = vT_ref[:, pl.ds(off, tk)]
        s1 = jnp.dot(kt, q1, preferred_element_type=F32)
        s2 = jnp.dot(kt, q2, preferred_element_type=F32)
        after = kmin_ref[i] - q_hi >= REL_MAX_DISTANCE
        before = kmax_ref[i] - q_lo <= -REL_MAX_DISTANCE

        def const_bias(s1, s2):
            c = jnp.where(after, b_after, b_before)
            return s1 + c, s2 + c

        def general_bias(s1, s2):
            pk = posk_ref[pl.ds(off, tk), :]
            pk = jnp.concatenate([pk] * (tq // LANE), axis=1) if tq > LANE else pk[:, :tq]
            bias = _t5_bias(pk - posq_ref[...], tab_ref, h)
            return s1 + bias, s2 + bias

        s1, s2 = lax.cond(jnp.logical_or(after, before), const_bias, general_bias, s1, s2)
        m1, l1 = _flash_update(s1, m1, l1, acc_ref, 0, vt)
        m2, l2 = _flash_update(s2, m2, l2, acc_ref, 1, vt)
        return m1, l1, m2, l2

    neg = jnp.full((1, tq), NEG_BIG, F32)
    zl = jnp.zeros((1, tq), F32)
    _, l1, _, l2 = lax.fori_loop(0, nk, body, (neg, zl, neg, zl))
    o = acc_ref[0] / l1 - lam_ref[0] * (acc_ref[1] / l2)
    ms = jnp.mean(o * o, axis=0, keepdims=True)
    o = o * lax.rsqrt(ms + DIFF_SUBLN_EPS) * g_ref[...]
    o_ref[...] = (o * out_scale).astype(o_ref.dtype)


def diff_attention(qT, qkv, vT, positions, rel_bias, lam, subln, lam_init, tq=256, tk=512):
    s = qkv.shape[0]
    tq, tk = _pick(s, tq), _pick(s, tk)
    nk = s // tk
    pos = positions.reshape(s).astype(jnp.int32)
    kmin = pos.reshape(nk, tk).min(axis=1)
    kmax = pos.reshape(nk, tk).max(axis=1)
    qmin = pos.reshape(s // tq, tq).min(axis=1)
    qmax = pos.reshape(s // tq, tq).max(axis=1)
    posq = pos.reshape(1, s)
    posk = jnp.broadcast_to(pos.reshape(s, 1), (s, LANE))
    smem = pl.BlockSpec(memory_space=pltpu.SMEM)
    grid_spec = pltpu.PrefetchScalarGridSpec(
        num_scalar_prefetch=4,
        grid=(DIFF_HEADS, s // tq),
        in_specs=[smem, smem,
                  pl.BlockSpec((DIFF_V_DIM, tq), lambda h, i, *_: (h, i)),
                  pl.BlockSpec((s, 2 * DIFF_HEAD_DIM), lambda h, i, *_: (0, DIFF_HEADS + h)),
                  pl.BlockSpec((DIFF_V_DIM, s), lambda h, i, *_: (h, 0)),
                  pl.BlockSpec((1, tq), lambda h, i, *_: (0, i)),
                  pl.BlockSpec((s, LANE), lambda h, i, *_: (0, 0)),
                  pl.BlockSpec((DIFF_V_DIM, 1), lambda h, i, *_: (0, 0))],
        out_specs=pl.BlockSpec((DIFF_V_DIM, tq), lambda h, i, *_: (h, i)),
        scratch_shapes=[pltpu.VMEM((2, DIFF_V_DIM, tq), F32)],
    )
    return pl.pallas_call(
        functools.partial(_diff_attn_body, tk=tk, nk=nk, out_scale=1.0 - lam_init),
        grid_spec=grid_spec,
        out_shape=jax.ShapeDtypeStruct((DIFF_HEADS * DIFF_V_DIM, s), BF16),
        compiler_params=_cparams(("parallel", "parallel")),
        name="diff_attention",
    )(kmin, kmax, qmin, qmax, rel_bias.T.astype(F32), lam.reshape(1).astype(F32),
      qT, qkv, vT, posq, posk, subln.reshape(DIFF_V_DIM, 1).astype(F32))


def _xa_body(x_ref, gpre_ref, wq_ref, kT_ref, v_ref, wo_ref, gpost_ref, o_ref):
    x = x_ref[...]
    h = (_rms(x, NORM_EPS) * gpre_ref[...]).astype(BF16)
    q = jnp.dot(h, wq_ref[...], preferred_element_type=F32) * (MEM_HEAD_DIM ** -0.5)
    q = q.astype(BF16)
    outs = []
    for hd in range(MEM_HEADS):
        sl = slice(hd * MEM_HEAD_DIM, (hd + 1) * MEM_HEAD_DIM)
        s = jnp.dot(q[:, sl], kT_ref[sl, :], preferred_element_type=F32)
        p = jnp.exp(s - jnp.max(s, axis=-1, keepdims=True))
        l = jnp.sum(p, axis=-1, keepdims=True)
        o = jnp.dot(p.astype(BF16), v_ref[:, sl], preferred_element_type=F32) / l
        outs.append(o.astype(BF16))
    o = jnp.concatenate(outs, axis=1)
    y = jnp.dot(o, wo_ref[...], preferred_element_type=F32)
    o_ref[...] = x + _rms(y, NORM_EPS) * gpost_ref[...]


def cross_attention_sublayer(x, g_pre, w_q, kT, v, w_o, g_post):
    s, d = x.shape
    ts = _pick(s, 256, 8)
    row = lambda i: (i, 0)
    fixed = lambda i: (0, 0)
    return pl.pallas_call(
        _xa_body,
        grid=(s // ts,),
        in_specs=[pl.BlockSpec((ts, d), row), pl.BlockSpec((1, d), fixed),
                  pl.BlockSpec(w_q.shape, fixed), pl.BlockSpec(kT.shape, fixed),
                  pl.BlockSpec(v.shape, fixed), pl.BlockSpec(w_o.shape, fixed),
                  pl.BlockSpec((1, d), fixed)],
        out_specs=pl.BlockSpec((ts, d), row),
        out_shape=jax.ShapeDtypeStruct((s, d), F32),
        compiler_params=_cparams(("parallel",)),
        name="cross_attention",
    )(x, g_pre.reshape(1, d), w_q, kT, v, w_o, g_post.reshape(1, d))


def _rot_cols(w):
    half = w.shape[-1] // 2
    return jnp.concatenate([-w[..., half:], w[..., :half]], axis=-1)


def _prep_w_in(w, q_rank, kv_rank):
    a = q_rank + kv_rank
    kr = w[:, a:a + MLA_ROPE_DIM]
    z = jnp.zeros_like(kr)
    w_lat = jnp.concatenate([w[:, :a], kr, z, _rot_cols(kr), z], axis=1)
    b = a + MLA_ROPE_DIM
    nd = 3 * DIFF_HEADS * DIFF_V_DIM
    return w_lat.astype(BF16), w[:, b:b + nd].astype(BF16), w[:, b + nd:].astype(BF16)


def _prep_w_uq(w):
    r = w.shape[0]
    w3 = w.reshape(r, MLA_HEADS, MLA_QK_DIM)
    nope, rope = w3[..., :MLA_NOPE_DIM], w3[..., MLA_NOPE_DIM:]
    z = jnp.zeros_like(rope)
    out = jnp.concatenate([nope, rope, z, _rot_cols(rope), z], axis=-1)
    return out.reshape(r, -1).astype(BF16)


def _prep_w_ukv(w):
    r = w.shape[0]
    w3 = w.reshape(r, MLA_HEADS, MLA_NOPE_DIM + MLA_V_DIM)
    k_nope = w3[..., :MLA_NOPE_DIM].reshape(r, -1)
    v = w3[..., MLA_NOPE_DIM:].reshape(r, -1)
    return jnp.concatenate([k_nope, v], axis=1).astype(BF16)


def _rope_tables(positions, s):
    half = MLA_ROPE_DIM // 2
    inv_freq = ROPE_BASE ** (-jnp.arange(half, dtype=F32) / half)
    ang = positions.reshape(s, 1).astype(F32) * inv_freq
    z = jnp.zeros((s, LANE - MLA_ROPE_DIM), F32)
    cos = jnp.concatenate([jnp.cos(ang), jnp.cos(ang), z], axis=1)
    sin = jnp.concatenate([jnp.sin(ang), jnp.sin(ang), z], axis=1)
    return cos, sin


def kernel(x, mem, positions, rel_bias, mix_norm_pre, mix_norm_post, w_in, mla_q_norm, mla_w_uq, mla_kv_norm, mla_w_ukv, diff_lambda, diff_subln, w_mla_branch, w_diff_branch, w_out, xa_norm_pre, xa_norm_post, xa_mem_norm, xa_w_q, xa_w_kv, xa_w_o, ffn_norm_pre, ffn_norm_post, ffn_w_in, ffn_w_out):
    b, s, d = x.shape
    assert b == 1, "kernel is written for batch 1"
    depth = w_in.shape[0]
    q_rank, kv_rank = mla_q_norm.shape[1], mla_kv_norm.shape[1]
    dff = ffn_w_out.shape[1]
    dff_pad = -(-dff // 512) * 512
    nh = DIFF_HEADS * DIFF_V_DIM

    xs = x.reshape(s, d)
    mems = mem.reshape(mem.shape[1], d)
    cos, sin = _rope_tables(positions, s)

    for l in range(depth):
        w_lat, w_qkvd, w_gate = _prep_w_in(w_in[l], q_rank, kv_rank)
        h = rmsnorm(xs, mix_norm_pre[l], BF16)
        lat = matmul(h, w_lat, F32, bn=768)
        qkvd = matmul(h, w_qkvd, BF16)
        gates = matmul(h, w_gate, BF16)

        q, k, v = mla_prep(lat, mla_q_norm[l], mla_kv_norm[l], _prep_w_uq(mla_w_uq[l]),
                           _prep_w_ukv(mla_w_ukv[l]), cos, sin)
        o_mla = mla_attention(q.T, k, v.T).T

        lam_init = 0.8 - 0.6 * math.exp(-0.3 * l)
        lv = diff_lambda[l].astype(F32)
        lam = jnp.exp(jnp.sum(lv[0] * lv[1])) - jnp.exp(jnp.sum(lv[2] * lv[3])) + lam_init
        qT_d = (qkvd[:, :nh] * (DIFF_HEAD_DIM ** -0.5)).T
        vT_d = qkvd[:, 2 * nh:].T
        o_diff = diff_attention(qT_d, qkvd, vT_d, positions, rel_bias, lam, diff_subln[l], lam_init).T

        merged = gated_merge(o_mla, o_diff, w_mla_branch[l].astype(BF16), w_diff_branch[l].astype(BF16), gates)
        y = matmul(merged, w_out[l].astype(BF16), F32)
        xs = add_rmsnorm(xs, y, mix_norm_post[l])

        mem_n = rmsnorm(mems, xa_mem_norm[l], BF16)
        kv = matmul(mem_n, xa_w_kv[l].astype(BF16), BF16)
        hm = MEM_HEADS * MEM_HEAD_DIM
        xs = cross_attention_sublayer(xs, xa_norm_pre[l], xa_w_q[l].astype(BF16), kv[:, :hm].T, kv[:, hm:],
                                      xa_w_o[l].astype(BF16), xa_norm_post[l])

        wi = ffn_w_in[l]
        pad = dff_pad - dff
        w_gu = jnp.concatenate([jnp.pad(wi[:, :dff], ((0, 0), (0, pad))),
                                jnp.pad(wi[:, dff:], ((0, 0), (0, pad)))], axis=1).astype(BF16)
        w_fo = jnp.pad(ffn_w_out[l], ((0, pad), (0, 0))).astype(BF16)
        h = rmsnorm(xs, ffn_norm_pre[l], BF16)
        act = swiglu_matmul(h, w_gu, dff_pad)
        y = matmul(act, w_fo, F32, bk=2816)
        xs = add_rmsnorm(xs, y, ffn_norm_post[l])

    return xs.reshape(b, s, d)
```

```python
import functools
import math

import jax
import jax.numpy as jnp
from jax import lax
from jax.experimental import pallas as pl
from jax.experimental.pallas import tpu as pltpu

F32 = jnp.float32
BF16 = jnp.bfloat16

MLA_HEADS = 8
MLA_NOPE_DIM = 128
MLA_ROPE_DIM = 64
MLA_V_DIM = 128
MLA_QK_DIM = MLA_NOPE_DIM + MLA_ROPE_DIM
DIFF_HEADS = 8
DIFF_HEAD_DIM = 64
DIFF_V_DIM = 2 * DIFF_HEAD_DIM
MEM_HEADS = 4
MEM_HEAD_DIM = 128
REL_BUCKETS = 32
REL_MAX_DISTANCE = 128
ROPE_BASE = 10000.0
NORM_EPS = 1e-6
DIFF_SUBLN_EPS = 1e-5

LANE = 128
VMEM_LIMIT_BYTES = 56 * 1024 * 1024
NEG_BIG = -1e30
LOG2E = math.log2(math.e)


def _cparams(sem):
    return pltpu.CompilerParams(dimension_semantics=sem, vmem_limit_bytes=VMEM_LIMIT_BYTES)


def _pick(dim, target, align=LANE):
    if dim <= target:
        return dim
    t = (target // align) * align
    while t >= align:
        if dim % t == 0:
            return t
        t -= align
    return dim


def _sigmoid(x):
    return 1.0 / (1.0 + jnp.exp(-x))


def _rms(x, eps):
    return x * lax.rsqrt(jnp.mean(x * x, axis=-1, keepdims=True) + eps)


def _rmsnorm_body(x_ref, g_ref, o_ref, *, eps):
    x = x_ref[...].astype(F32)
    o_ref[...] = (_rms(x, eps) * g_ref[...]).astype(o_ref.dtype)


def rmsnorm(x, g, out_dtype, eps=NORM_EPS):
    m, d = x.shape
    bm = _pick(m, 256, 8)
    return pl.pallas_call(
        functools.partial(_rmsnorm_body, eps=eps),
        grid=(m // bm,),
        in_specs=[pl.BlockSpec((bm, d), lambda i: (i, 0)), pl.BlockSpec((1, d), lambda i: (0, 0))],
        out_specs=pl.BlockSpec((bm, d), lambda i: (i, 0)),
        out_shape=jax.ShapeDtypeStruct((m, d), out_dtype),
        compiler_params=_cparams(("parallel",)),
        name="rmsnorm",
    )(x, g.reshape(1, d))


def _add_rmsnorm_body(x_ref, y_ref, g_ref, o_ref, *, eps):
    y = y_ref[...].astype(F32)
    o_ref[...] = x_ref[...] + _rms(y, eps) * g_ref[...]


def add_rmsnorm(x, y, g, eps=NORM_EPS):
    m, d = x.shape
    bm = _pick(m, 256, 8)
    return pl.pallas_call(
        functools.partial(_add_rmsnorm_body, eps=eps),
        grid=(m // bm,),
        in_specs=[pl.BlockSpec((bm, d), lambda i: (i, 0)), pl.BlockSpec((bm, d), lambda i: (i, 0)),
                  pl.BlockSpec((1, d), lambda i: (0, 0))],
        out_specs=pl.BlockSpec((bm, d), lambda i: (i, 0)),
        out_shape=jax.ShapeDtypeStruct((m, d), F32),
        compiler_params=_cparams(("parallel",)),
        name="add_rmsnorm",
    )(x, y, g.reshape(1, d))


def _matmul_body(a_ref, w_ref, o_ref, *scratch, nk):
    part = jnp.dot(a_ref[...], w_ref[...], preferred_element_type=F32)
    if nk == 1:
        o_ref[...] = part.astype(o_ref.dtype)
        return
    (acc_ref,) = scratch
    k = pl.program_id(2)

    @pl.when(k == 0)
    def _():
        acc_ref[...] = part

    @pl.when(k > 0)
    def _():
        acc_ref[...] += part

    @pl.when(k == nk - 1)
    def _():
        o_ref[...] = acc_ref[...].astype(o_ref.dtype)


def matmul(a, w, out_dtype, bm=1024, bn=1024, bk=4096):
    m, kd = a.shape
    _, n = w.shape
    bm, bn, bk = _pick(m, bm, 8), _pick(n, bn), _pick(kd, bk)
    nk = kd // bk
    return pl.pallas_call(
        functools.partial(_matmul_body, nk=nk),
        grid=(m // bm, n // bn, nk),
        in_specs=[pl.BlockSpec((bm, bk), lambda i, j, k: (i, k)),
                  pl.BlockSpec((bk, bn), lambda i, j, k: (k, j))],
        out_specs=pl.BlockSpec((bm, bn), lambda i, j, k: (i, j)),
        out_shape=jax.ShapeDtypeStruct((m, n), out_dtype),
        scratch_shapes=[pltpu.VMEM((bm, bn), F32)] if nk > 1 else [],
        compiler_params=_cparams(("parallel", "parallel", "arbitrary")),
        name="matmul",
    )(a, w)


def _swiglu_body(a_ref, wg_ref, wu_ref, o_ref):
    a = a_ref[...]
    g = jnp.dot(a, wg_ref[...], preferred_element_type=F32)
    u = jnp.dot(a, wu_ref[...], preferred_element_type=F32)
    o_ref[...] = (g * _sigmoid(g) * u).astype(o_ref.dtype)


def swiglu_matmul(a, w_gu, dff, bm=1024, bn=512):
    m, kd = a.shape
    bm, bn = _pick(m, bm, 8), _pick(dff, bn)
    nj = dff // bn
    return pl.pallas_call(
        _swiglu_body,
        grid=(m // bm, nj),
        in_specs=[pl.BlockSpec((bm, kd), lambda i, j: (i, 0)),
                  pl.BlockSpec((kd, bn), lambda i, j: (0, j)),
                  pl.BlockSpec((kd, bn), lambda i, j: (0, j + nj))],
        out_specs=pl.BlockSpec((bm, bn), lambda i, j: (i, j)),
        out_shape=jax.ShapeDtypeStruct((m, dff), BF16),
        compiler_params=_cparams(("parallel", "parallel")),
        name="swiglu_matmul",
    )(a, w_gu, w_gu)


def _merge_body(om_ref, od_ref, wm_ref, wd_ref, gm_ref, gd_ref, o_ref):
    ym = jnp.dot(om_ref[...], wm_ref[...], preferred_element_type=F32)
    yd = jnp.dot(od_ref[...], wd_ref[...], preferred_element_type=F32)
    sm = _sigmoid(gm_ref[...].astype(F32))
    sd = _sigmoid(gd_ref[...].astype(F32))
    o_ref[...] = (sm * ym + sd * yd).astype(o_ref.dtype)


def gated_merge(o_mla, o_diff, w_mla, w_diff, gates, bm=1024, bn=1024):
    m, km = o_mla.shape
    _, kd = o_diff.shape
    _, n = w_mla.shape
    bm, bn = _pick(m, bm, 8), _pick(n, bn)
    nj = n // bn
    return pl.pallas_call(
        _merge_body,
        grid=(m // bm, nj),
        in_specs=[pl.BlockSpec((bm, km), lambda i, j: (i, 0)),
                  pl.BlockSpec((bm, kd), lambda i, j: (i, 0)),
                  pl.BlockSpec((km, bn), lambda i, j: (0, j)),
                  pl.BlockSpec((kd, bn), lambda i, j: (0, j)),
                  pl.BlockSpec((bm, bn), lambda i, j: (i, j)),
                  pl.BlockSpec((bm, bn), lambda i, j: (i, j + nj))],
        out_specs=pl.BlockSpec((bm, bn), lambda i, j: (i, j)),
        out_shape=jax.ShapeDtypeStruct((m, n), BF16),
        compiler_params=_cparams(("parallel", "parallel")),
        name="gated_merge",
    )(o_mla, o_diff, w_mla, w_diff, gates, gates)


def _mla_prep_body(lat_ref, gq_ref, gkv_ref, wq_ref, wkv_ref, cos_ref, sin_ref,
                   q_ref, k_ref, v_ref, *, q_rank, kv_rank, scale):
    cos = cos_ref[...]
    sin = sin_ref[...]
    cq = lat_ref[:, :q_rank].astype(F32)
    ckv = lat_ref[:, q_rank:q_rank + kv_rank].astype(F32)
    kr = lat_ref[:, q_rank + kv_rank:q_rank + kv_rank + LANE].astype(F32)
    kr_rot = lat_ref[:, q_rank + kv_rank + LANE:q_rank + kv_rank + 2 * LANE].astype(F32)
    k_rope = (kr * cos + kr_rot * sin).astype(BF16)

    cqn = (_rms(cq, NORM_EPS) * gq_ref[...]).astype(BF16)
    qa = jnp.dot(cqn, wq_ref[...], preferred_element_type=F32)
    ckvn = (_rms(ckv, NORM_EPS) * gkv_ref[...]).astype(BF16)
    kv = jnp.dot(ckvn, wkv_ref[...], preferred_element_type=F32)

    hw = MLA_NOPE_DIM + 2 * LANE
    for h in range(MLA_HEADS):
        base = h * hw
        nope = qa[:, base:base + MLA_NOPE_DIM]
        rp = qa[:, base + MLA_NOPE_DIM:base + MLA_NOPE_DIM + LANE]
        rr = qa[:, base + MLA_NOPE_DIM + LANE:base + hw]
        q_ref[:, h * 256:h * 256 + 128] = (nope * scale).astype(BF16)
        q_ref[:, h * 256 + 128:(h + 1) * 256] = ((rp * cos + rr * sin) * scale).astype(BF16)
        k_ref[:, h * 256:h * 256 + 128] = kv[:, h * 128:(h + 1) * 128].astype(BF16)
        k_ref[:, h * 256 + 128:(h + 1) * 256] = k_rope
    v_ref[...] = kv[:, MLA_HEADS * MLA_NOPE_DIM:].astype(BF16)


def mla_prep(lat, gq, gkv, wq, wkv, cos, sin):
    s, lw = lat.shape
    q_rank, kv_rank = gq.shape[0], gkv.shape[0]
    ts = _pick(s, 256, 8)
    hq = MLA_HEADS * 256
    hv = MLA_HEADS * MLA_V_DIM
    row = lambda i: (i, 0)
    fixed = lambda i: (0, 0)
    return pl.pallas_call(
        functools.partial(_mla_prep_body, q_rank=q_rank, kv_rank=kv_rank, scale=LOG2E * MLA_QK_DIM ** -0.5),
        grid=(s // ts,),
        in_specs=[pl.BlockSpec((ts, lw), row),
                  pl.BlockSpec((1, q_rank), fixed), pl.BlockSpec((1, kv_rank), fixed),
                  pl.BlockSpec(wq.shape, fixed), pl.BlockSpec(wkv.shape, fixed),
                  pl.BlockSpec((ts, LANE), row), pl.BlockSpec((ts, LANE), row)],
        out_specs=[pl.BlockSpec((ts, hq), row), pl.BlockSpec((ts, hq), row), pl.BlockSpec((ts, hv), row)],
        out_shape=[jax.ShapeDtypeStruct((s, hq), BF16), jax.ShapeDtypeStruct((s, hq), BF16),
                   jax.ShapeDtypeStruct((s, hv), BF16)],
        compiler_params=_cparams(("parallel",)),
        name="mla_prep",
    )(lat, gq.reshape(1, -1), gkv.reshape(1, -1), wq, wkv, cos, sin)


CHAIN_W = 256


def _chain_step(ss, c, ms, ls, acc_ref, vt):
    n = len(ss)
    tmax = [jnp.max(s, axis=0, keepdims=True) for s in ss]
    if c is not None:
        tmax = [t + c for t in tmax]
    m_new = [jnp.maximum(ms[j], tmax[j]) for j in range(n)]
    shift = m_new if c is None else [m - c for m in m_new]
    ps = [jnp.exp2(ss[j] - shift[j]) for j in range(n)]
    alpha = [jnp.exp2(ms[j] - m_new[j]) for j in range(n)]
    l_new = [alpha[j] * ls[j] + jnp.sum(ps[j], axis=0, keepdims=True) for j in range(n)]
    for j in range(n):
        pv = jnp.dot(vt, ps[j].astype(BF16), preferred_element_type=F32)
        acc_ref[j] = alpha[j] * acc_ref[j] + pv
    return m_new, l_new


def _mla_attn_body(qT_ref, k_ref, vT_ref, o_ref, acc_ref, *, tk, nk, nc):
    acc_ref[...] = jnp.zeros_like(acc_ref)

    def body(i, carry):
        ms, ls = carry
        off = pl.multiple_of(i * tk, tk)
        kt = k_ref[pl.ds(off, tk), :]
        vt = vT_ref[:, pl.ds(off, tk)]
        ss = [jnp.dot(kt, qT_ref[:, c * CHAIN_W:(c + 1) * CHAIN_W], preferred_element_type=F32)
              for c in range(nc)]
        ms, ls = _chain_step(ss, None, ms, ls, acc_ref, vt)
        return tuple(ms), tuple(ls)

    neg = jnp.full((1, CHAIN_W), NEG_BIG, F32)
    zl = jnp.zeros((1, CHAIN_W), F32)
    _, ls = lax.fori_loop(0, nk, body, ((neg,) * nc, (zl,) * nc))
    for c in range(nc):
        o_ref[:, c * CHAIN_W:(c + 1) * CHAIN_W] = (acc_ref[c] / ls[c]).astype(o_ref.dtype)


def mla_attention(qT, k, vT, nc=8, tk=1024):
    s = k.shape[0]
    tq = _pick(s, nc * CHAIN_W, CHAIN_W)
    nc = tq // CHAIN_W
    tk = _pick(s, tk)
    nk = s // tk
    return pl.pallas_call(
        functools.partial(_mla_attn_body, tk=tk, nk=nk, nc=nc),
        grid=(MLA_HEADS, s // tq),
        in_specs=[pl.BlockSpec((256, tq), lambda h, i: (h, i)),
                  pl.BlockSpec((s, 256), lambda h, i: (0, h)),
                  pl.BlockSpec((MLA_V_DIM, s), lambda h, i: (h, 0))],
        out_specs=pl.BlockSpec((MLA_V_DIM, tq), lambda h, i: (h, i)),
        out_shape=jax.ShapeDtypeStruct((MLA_HEADS * MLA_V_DIM, s), BF16),
        scratch_shapes=[pltpu.VMEM((nc, MLA_V_DIM, CHAIN_W), F32)],
        compiler_params=_cparams(("parallel", "parallel")),
        name="mla_attention",
    )(qT, k, vT)


T5_BUCKET_EDGES = (0, 1, 2, 3, 4, 5, 6, 7, 8, 12, 16, 23, 32, 46, 64, 91)


def _t5_bias(rel, tab_ref, h):
    nb = REL_BUCKETS // 2
    pos = rel > 0
    n = jnp.abs(rel)
    v = jnp.where(pos, tab_ref[h, nb], tab_ref[h, 0])
    for b in range(1, nb):
        leaf = jnp.where(pos, tab_ref[h, nb + b], tab_ref[h, b])
        v = jnp.where(n >= T5_BUCKET_EDGES[b], leaf, v)
    return v


def _diff_attn_body(kmin_ref, kmax_ref, smin_ref, smax_ref, qmin_ref, qmax_ref, cmin_ref, cmax_ref,
                    tab_ref, lam_ref, qT_ref, k_ref, vT_ref, posq_ref, posk_ref, g_ref,
                    o_ref, qm_ref, acc_ref, bias_ref, *, tk, nk, nc, sk, out_scale):
    h = pl.program_id(0)
    qi = pl.program_id(1)
    nch = 2 * nc
    qT = qT_ref[...]
    row = lax.broadcasted_iota(jnp.int32, qT.shape, 0)
    zero = jnp.zeros_like(qT)
    qm_ref[0] = jnp.where(row < DIFF_HEAD_DIM, qT, zero)
    qm_ref[1] = jnp.where(row >= DIFF_HEAD_DIM, qT, zero)
    b_after = tab_ref[h, REL_BUCKETS - 1]
    b_before = tab_ref[h, REL_BUCKETS // 2 - 1]
    q_lo = qmin_ref[qi]
    q_hi = qmax_ref[qi]
    acc_ref[...] = jnp.zeros_like(acc_ref)
    nsub = tk // sk

    def scores(kt):
        return [jnp.dot(kt, qm_ref[j % 2, :, (j // 2) * CHAIN_W:(j // 2 + 1) * CHAIN_W],
                        preferred_element_type=F32) for j in range(nch)]

    def body(i, carry):
        off = pl.multiple_of(i * tk, tk)
        after = kmin_ref[i] - q_hi >= REL_MAX_DISTANCE
        before = kmax_ref[i] - q_lo <= -REL_MAX_DISTANCE

        def far_tile(carry):
            ms, ls = carry
            kt = k_ref[pl.ds(off, tk), :]
            vt = vT_ref[:, pl.ds(off, tk)]
            c = jnp.where(after, b_after, b_before)
            ms, ls = _chain_step(scores(kt), c, ms, ls, acc_ref, vt)
            return tuple(ms), tuple(ls)

        def near_tile(carry):
            ms, ls = carry

            def fill(n, _):
                t, cc = n // nc, n % nc
                u = i * nsub + t
                cq = qi * nc + cc
                blk_after = smin_ref[u] - cmax_ref[cq] >= REL_MAX_DISTANCE
                blk_before = smax_ref[u] - cmin_ref[cq] <= -REL_MAX_DISTANCE
                roff = pl.multiple_of(t * sk, sk)
                coff = pl.multiple_of(cc * CHAIN_W, CHAIN_W)

                @pl.when(jnp.logical_or(blk_after, blk_before))
                def _():
                    c = jnp.where(blk_after, b_after, b_before)
                    bias_ref[pl.ds(roff, sk), pl.ds(coff, CHAIN_W)] = jnp.full((sk, CHAIN_W), c, F32)

                @pl.when(jnp.logical_not(jnp.logical_or(blk_after, blk_before)))
                def _():
                    pk = posk_ref[pl.ds(pl.multiple_of(off + roff, sk), sk), :]
                    pk = jnp.concatenate([pk] * (CHAIN_W // LANE), axis=1)
                    rel = pk - posq_ref[:, pl.ds(coff, CHAIN_W)]
                    bias_ref[pl.ds(roff, sk), pl.ds(coff, CHAIN_W)] = _t5_bias(rel, tab_ref, h)

                return 0

            lax.fori_loop(0, nsub * nc, fill, 0)
            kt = k_ref[pl.ds(off, tk), :]
            vt = vT_ref[:, pl.ds(off, tk)]
            ss = scores(kt)
            ss = [ss[j] + bias_ref[:, (j // 2) * CHAIN_W:(j // 2 + 1) * CHAIN_W] for j in range(nch)]
            ms, ls = _chain_step(ss, None, ms, ls, acc_ref, vt)
            return tuple(ms), tuple(ls)

        return lax.cond(jnp.logical_or(after, before), far_tile, near_tile, carry)

    neg = jnp.full((1, CHAIN_W), NEG_BIG, F32)
    zl = jnp.zeros((1, CHAIN_W), F32)
    _, ls = lax.fori_loop(0, nk, body, ((neg,) * nch, (zl,) * nch))
    lam = lam_ref[0]
    for c in range(nc):
        o = acc_ref[2 * c] / ls[2 * c] - lam * (acc_ref[2 * c + 1] / ls[2 * c + 1])
        ms = jnp.mean(o * o, axis=0, keepdims=True)
        o = o * lax.rsqrt(ms + DIFF_SUBLN_EPS) * g_ref[...]
        o_ref[:, c * CHAIN_W:(c + 1) * CHAIN_W] = (o * out_scale).astype(o_ref.dtype)


def diff_attention(qT, qkv, vT, positions, rel_bias, lam, subln, lam_init, nc=4, tk=1024, sk=128):
    s = qkv.shape[0]
    tq = _pick(s, nc * CHAIN_W, CHAIN_W)
    nc = tq // CHAIN_W
    tk = _pick(s, tk)
    sk = _pick(tk, sk)
    nk = s // tk
    pos = positions.reshape(s).astype(jnp.int32)
    kmin = pos.reshape(nk, tk).min(axis=1)
    kmax = pos.reshape(nk, tk).max(axis=1)
    smin = pos.reshape(s // sk, sk).min(axis=1)
    smax = pos.reshape(s // sk, sk).max(axis=1)
    qmin = pos.reshape(s // tq, tq).min(axis=1)
    qmax = pos.reshape(s // tq, tq).max(axis=1)
    cmin = pos.reshape(s // CHAIN_W, CHAIN_W).min(axis=1)
    cmax = pos.reshape(s // CHAIN_W, CHAIN_W).max(axis=1)
    posq = pos.reshape(1, s)
    posk = jnp.broadcast_to(pos.reshape(s, 1), (s, LANE))
    tab = (rel_bias.T * LOG2E).astype(F32)
    smem = pl.BlockSpec(memory_space=pltpu.SMEM)
    grid_spec = pltpu.PrefetchScalarGridSpec(
        num_scalar_prefetch=8,
        grid=(DIFF_HEADS, s // tq),
        in_specs=[smem, smem,
                  pl.BlockSpec((DIFF_V_DIM, tq), lambda h, i, *_: (h, i)),
                  pl.BlockSpec((s, 2 * DIFF_HEAD_DIM), lambda h, i, *_: (0, DIFF_HEADS + h)),
                  pl.BlockSpec((DIFF_V_DIM, s), lambda h, i, *_: (h, 0)),
                  pl.BlockSpec((1, tq), lambda h, i, *_: (0, i)),
                  pl.BlockSpec((s, LANE), lambda h, i, *_: (0, 0)),
                  pl.BlockSpec((DIFF_V_DIM, 1), lambda h, i, *_: (0, 0))],
        out_specs=pl.BlockSpec((DIFF_V_DIM, tq), lambda h, i, *_: (h, i)),
        scratch_shapes=[pltpu.VMEM((2, DIFF_V_DIM, tq), BF16),
                        pltpu.VMEM((2 * nc, DIFF_V_DIM, CHAIN_W), F32),
                        pltpu.VMEM((tk, tq), F32)],
    )
    return pl.pallas_call(
        functools.partial(_diff_attn_body, tk=tk, nk=nk, nc=nc, sk=sk, out_scale=1.0 - lam_init),
        grid_spec=grid_spec,
        out_shape=jax.ShapeDtypeStruct((DIFF_HEADS * DIFF_V_DIM, s), BF16),
        compiler_params=_cparams(("parallel", "parallel")),
        name="diff_attention",
    )(kmin, kmax, smin, smax, qmin, qmax, cmin, cmax, tab, lam.reshape(1).astype(F32),
      qT, qkv, vT, posq, posk, subln.reshape(DIFF_V_DIM, 1).astype(F32))


def _xa_body(x_ref, gpre_ref, wq_ref, kT_ref, v_ref, wo_ref, gpost_ref, o_ref):
    x = x_ref[...]
    h = (_rms(x, NORM_EPS) * gpre_ref[...]).astype(BF16)
    q = jnp.dot(h, wq_ref[...], preferred_element_type=F32) * (MEM_HEAD_DIM ** -0.5)
    q = q.astype(BF16)
    outs = []
    for hd in range(MEM_HEADS):
        sl = slice(hd * MEM_HEAD_DIM, (hd + 1) * MEM_HEAD_DIM)
        s = jnp.dot(q[:, sl], kT_ref[sl, :], preferred_element_type=F32)
        p = jnp.exp(s - jnp.max(s, axis=-1, keepdims=True))
        l = jnp.sum(p, axis=-1, keepdims=True)
        o = jnp.dot(p.astype(BF16), v_ref[:, sl], preferred_element_type=F32) / l
        outs.append(o.astype(BF16))
    o = jnp.concatenate(outs, axis=1)
    y = jnp.dot(o, wo_ref[...], preferred_element_type=F32)
    o_ref[...] = x + _rms(y, NORM_EPS) * gpost_ref[...]


def cross_attention_sublayer(x, g_pre, w_q, kT, v, w_o, g_post):
    s, d = x.shape
    ts = _pick(s, 256, 8)
    row = lambda i: (i, 0)
    fixed = lambda i: (0, 0)
    return pl.pallas_call(
        _xa_body,
        grid=(s // ts,),
        in_specs=[pl.BlockSpec((ts, d), row), pl.BlockSpec((1, d), fixed),
                  pl.BlockSpec(w_q.shape, fixed), pl.BlockSpec(kT.shape, fixed),
                  pl.BlockSpec(v.shape, fixed), pl.BlockSpec(w_o.shape, fixed),
                  pl.BlockSpec((1, d), fixed)],
        out_specs=pl.BlockSpec((ts, d), row),
        out_shape=jax.ShapeDtypeStruct((s, d), F32),
        compiler_params=_cparams(("parallel",)),
        name="cross_attention",
    )(x, g_pre.reshape(1, d), w_q, kT, v, w_o, g_post.reshape(1, d))


def _rot_cols(w):
    half = w.shape[-1] // 2
    return jnp.concatenate([-w[..., half:], w[..., :half]], axis=-1)


def _prep_w_in(w, q_rank, kv_rank):
    a = q_rank + kv_rank
    kr = w[:, a:a + MLA_ROPE_DIM]
    z = jnp.zeros_like(kr)
    w_lat = jnp.concatenate([w[:, :a], kr, z, _rot_cols(kr), z], axis=1)
    b = a + MLA_ROPE_DIM
    nq = DIFF_HEADS * DIFF_V_DIM
    w_q = w[:, b:b + nq] * (LOG2E * DIFF_HEAD_DIM ** -0.5)
    w_qkvd = jnp.concatenate([w_q, w[:, b + nq:b + 3 * nq]], axis=1)
    return w_lat.astype(BF16), w_qkvd.astype(BF16), w[:, b + 3 * nq:].astype(BF16)


def _prep_w_uq(w):
    r = w.shape[0]
    w3 = w.reshape(r, MLA_HEADS, MLA_QK_DIM)
    nope, rope = w3[..., :MLA_NOPE_DIM], w3[..., MLA_NOPE_DIM:]
    z = jnp.zeros_like(rope)
    out = jnp.concatenate([nope, rope, z, _rot_cols(rope), z], axis=-1)
    return out.reshape(r, -1).astype(BF16)


def _prep_w_ukv(w):
    r = w.shape[0]
    w3 = w.reshape(r, MLA_HEADS, MLA_NOPE_DIM + MLA_V_DIM)
    k_nope = w3[..., :MLA_NOPE_DIM].reshape(r, -1)
    v = w3[..., MLA_NOPE_DIM:].reshape(r, -1)
    return jnp.concatenate([k_nope, v], axis=1).astype(BF16)


def _rope_tables(positions, s):
    half = MLA_ROPE_DIM // 2
    inv_freq = ROPE_BASE ** (-jnp.arange(half, dtype=F32) / half)
    ang = positions.reshape(s, 1).astype(F32) * inv_freq
    z = jnp.zeros((s, LANE - MLA_ROPE_DIM), F32)
    cos = jnp.concatenate([jnp.cos(ang), jnp.cos(ang), z], axis=1)
    sin = jnp.concatenate([jnp.sin(ang), jnp.sin(ang), z], axis=1)
    return cos, sin


def kernel(x, mem, positions, rel_bias, mix_norm_pre, mix_norm_post, w_in, mla_q_norm, mla_w_uq, mla_kv_norm, mla_w_ukv, diff_lambda, diff_subln, w_mla_branch, w_diff_branch, w_out, xa_norm_pre, xa_norm_post, xa_mem_norm, xa_w_q, xa_w_kv, xa_w_o, ffn_norm_pre, ffn_norm_post, ffn_w_in, ffn_w_out):
    b, s, d = x.shape
    assert b == 1, "kernel is written for batch 1"
    depth = w_in.shape[0]
    q_rank, kv_rank = mla_q_norm.shape[1], mla_kv_norm.shape[1]
    dff = ffn_w_out.shape[1]
    dff_pad = -(-dff // 512) * 512
    nh = DIFF_HEADS * DIFF_V_DIM

    xs = x.reshape(s, d)
    mems = mem.reshape(mem.shape[1], d)
    cos, sin = _rope_tables(positions, s)

    for l in range(depth):
        w_lat, w_qkvd, w_gate = _prep_w_in(w_in[l], q_rank, kv_rank)
        h = rmsnorm(xs, mix_norm_pre[l], BF16)
        lat = matmul(h, w_lat, F32, bn=768)
        qkvd = matmul(h, w_qkvd, BF16)
        gates = matmul(h, w_gate, BF16)

        q, k, v = mla_prep(lat, mla_q_norm[l], mla_kv_norm[l], _prep_w_uq(mla_w_uq[l]),
                           _prep_w_ukv(mla_w_ukv[l]), cos, sin)
        o_mla = mla_attention(q.T, k, v.T).T

        lam_init = 0.8 - 0.6 * math.exp(-0.3 * l)
        lv = diff_lambda[l].astype(F32)
        lam = jnp.exp(jnp.sum(lv[0] * lv[1])) - jnp.exp(jnp.sum(lv[2] * lv[3])) + lam_init
        qT_d = qkvd[:, :nh].T
        vT_d = qkvd[:, 2 * nh:].T
        o_diff = diff_attention(qT_d, qkvd, vT_d, positions, rel_bias, lam, diff_subln[l], lam_init).T

        merged = gated_merge(o_mla, o_diff, w_mla_branch[l].astype(BF16), w_diff_branch[l].astype(BF16), gates)
        y = matmul(merged, w_out[l].astype(BF16), F32)
        xs = add_rmsnorm(xs, y, mix_norm_post[l])

        mem_n = rmsnorm(mems, xa_mem_norm[l], BF16)
        kv = matmul(mem_n, xa_w_kv[l].astype(BF16), BF16)
        hm = MEM_HEADS * MEM_HEAD_DIM
        xs = cross_attention_sublayer(xs, xa_norm_pre[l], xa_w_q[l].astype(BF16), kv[:, :hm].T, kv[:, hm:],
                                      xa_w_o[l].astype(BF16), xa_norm_post[l])

        wi = ffn_w_in[l]
        pad = dff_pad - dff
        w_gu = jnp.concatenate([jnp.pad(wi[:, :dff], ((0, 0), (0, pad))),
                                jnp.pad(wi[:, dff:], ((0, 0), (0, pad)))], axis=1).astype(BF16)
        w_fo = jnp.pad(ffn_w_out[l], ((0, pad), (0, 0))).astype(BF16)
        h = rmsnorm(xs, ffn_norm_pre[l], BF16)
        act = swiglu_matmul(h, w_gu, dff_pad)
        y = matmul(act, w_fo, F32, bk=2816)
        xs = add_rmsnorm(xs, y, ffn_norm_post[l])

    return xs.reshape(b, s, d)
```

```python
import functools
import math

import jax
import jax.numpy as jnp
from jax import lax
from jax.experimental import pallas as pl
from jax.experimental.pallas import tpu as pltpu

F32 = jnp.float32
BF16 = jnp.bfloat16

MLA_HEADS = 8
MLA_NOPE_DIM = 128
MLA_ROPE_DIM = 64
MLA_V_DIM = 128
MLA_QK_DIM = MLA_NOPE_DIM + MLA_ROPE_DIM
DIFF_HEADS = 8
DIFF_HEAD_DIM = 64
DIFF_V_DIM = 2 * DIFF_HEAD_DIM
MEM_HEADS = 4
MEM_HEAD_DIM = 128
REL_BUCKETS = 32
REL_MAX_DISTANCE = 128
ROPE_BASE = 10000.0
NORM_EPS = 1e-6
DIFF_SUBLN_EPS = 1e-5

LANE = 128
VMEM_LIMIT_BYTES = 56 * 1024 * 1024
NEG_BIG = -1e30
LOG2E = math.log2(math.e)


def _cparams(sem):
    return pltpu.CompilerParams(dimension_semantics=sem, vmem_limit_bytes=VMEM_LIMIT_BYTES)


def _pick(dim, target, align=LANE):
    if dim <= target:
        return dim
    t = (target // align) * align
    while t >= align:
        if dim % t == 0:
            return t
        t -= align
    return dim


def _sigmoid(x):
    return 1.0 / (1.0 + jnp.exp(-x))


def _rms(x, eps):
    return x * lax.rsqrt(jnp.mean(x * x, axis=-1, keepdims=True) + eps)


def _rmsnorm_body(x_ref, g_ref, o_ref, *, eps):
    x = x_ref[...].astype(F32)
    o_ref[...] = (_rms(x, eps) * g_ref[...]).astype(o_ref.dtype)


def rmsnorm(x, g, out_dtype, eps=NORM_EPS):
    m, d = x.shape
    bm = _pick(m, 256, 8)
    return pl.pallas_call(
        functools.partial(_rmsnorm_body, eps=eps),
        grid=(m // bm,),
        in_specs=[pl.BlockSpec((bm, d), lambda i: (i, 0)), pl.BlockSpec((1, d), lambda i: (0, 0))],
        out_specs=pl.BlockSpec((bm, d), lambda i: (i, 0)),
        out_shape=jax.ShapeDtypeStruct((m, d), out_dtype),
        compiler_params=_cparams(("parallel",)),
        name="rmsnorm",
    )(x, g.reshape(1, d))


def _add_rmsnorm_body(x_ref, y_ref, g_ref, o_ref, *, eps):
    y = y_ref[...].astype(F32)
    o_ref[...] = x_ref[...] + _rms(y, eps) * g_ref[...]


def add_rmsnorm(x, y, g, eps=NORM_EPS):
    m, d = x.shape
    bm = _pick(m, 256, 8)
    return pl.pallas_call(
        functools.partial(_add_rmsnorm_body, eps=eps),
        grid=(m // bm,),
        in_specs=[pl.BlockSpec((bm, d), lambda i: (i, 0)), pl.BlockSpec((bm, d), lambda i: (i, 0)),
                  pl.BlockSpec((1, d), lambda i: (0, 0))],
        out_specs=pl.BlockSpec((bm, d), lambda i: (i, 0)),
        out_shape=jax.ShapeDtypeStruct((m, d), F32),
        compiler_params=_cparams(("parallel",)),
        name="add_rmsnorm",
    )(x, y, g.reshape(1, d))


def _matmul_body(a_ref, w_ref, o_ref, *scratch, nk):
    part = jnp.dot(a_ref[...], w_ref[...], preferred_element_type=F32)
    if nk == 1:
        o_ref[...] = part.astype(o_ref.dtype)
        return
    (acc_ref,) = scratch
    k = pl.program_id(2)

    @pl.when(k == 0)
    def _():
        acc_ref[...] = part

    @pl.when(k > 0)
    def _():
        acc_ref[...] += part

    @pl.when(k == nk - 1)
    def _():
        o_ref[...] = acc_ref[...].astype(o_ref.dtype)


def matmul(a, w, out_dtype, bm=1024, bn=1024, bk=4096):
    m, kd = a.shape
    _, n = w.shape
    bm, bn, bk = _pick(m, bm, 8), _pick(n, bn), _pick(kd, bk)
    nk = kd // bk
    return pl.pallas_call(
        functools.partial(_matmul_body, nk=nk),
        grid=(m // bm, n // bn, nk),
        in_specs=[pl.BlockSpec((bm, bk), lambda i, j, k: (i, k)),
                  pl.BlockSpec((bk, bn), lambda i, j, k: (k, j))],
        out_specs=pl.BlockSpec((bm, bn), lambda i, j, k: (i, j)),
        out_shape=jax.ShapeDtypeStruct((m, n), out_dtype),
        scratch_shapes=[pltpu.VMEM((bm, bn), F32)] if nk > 1 else [],
        compiler_params=_cparams(("parallel", "parallel", "arbitrary")),
        name="matmul",
    )(a, w)


def _swiglu_body(a_ref, wg_ref, wu_ref, o_ref):
    a = a_ref[...]
    g = jnp.dot(a, wg_ref[...], preferred_element_type=F32)
    u = jnp.dot(a, wu_ref[...], preferred_element_type=F32)
    o_ref[...] = (g * _sigmoid(g) * u).astype(o_ref.dtype)


def swiglu_matmul(a, w_gu, dff, bm=1024, bn=512):
    m, kd = a.shape
    bm, bn = _pick(m, bm, 8), _pick(dff, bn)
    nj = dff // bn
    return pl.pallas_call(
        _swiglu_body,
        grid=(m // bm, nj),
        in_specs=[pl.BlockSpec((bm, kd), lambda i, j: (i, 0)),
                  pl.BlockSpec((kd, bn), lambda i, j: (0, j)),
                  pl.BlockSpec((kd, bn), lambda i, j: (0, j + nj))],
        out_specs=pl.BlockSpec((bm, bn), lambda i, j: (i, j)),
        out_shape=jax.ShapeDtypeStruct((m, dff), BF16),
        compiler_params=_cparams(("parallel", "parallel")),
        name="swiglu_matmul",
    )(a, w_gu, w_gu)


def _merge_body(om_ref, od_ref, wm_ref, wd_ref, gm_ref, gd_ref, o_ref):
    ym = jnp.dot(om_ref[...], wm_ref[...], preferred_element_type=F32)
    yd = jnp.dot(od_ref[...], wd_ref[...], preferred_element_type=F32)
    sm = _sigmoid(gm_ref[...].astype(F32))
    sd = _sigmoid(gd_ref[...].astype(F32))
    o_ref[...] = (sm * ym + sd * yd).astype(o_ref.dtype)


def gated_merge(o_mla, o_diff, w_mla, w_diff, gates, bm=1024, bn=1024):
    m, km = o_mla.shape
    _, kd = o_diff.shape
    _, n = w_mla.shape
    bm, bn = _pick(m, bm, 8), _pick(n, bn)
    nj = n // bn
    return pl.pallas_call(
        _merge_body,
        grid=(m // bm, nj),
        in_specs=[pl.BlockSpec((bm, km), lambda i, j: (i, 0)),
                  pl.BlockSpec((bm, kd), lambda i, j: (i, 0)),
                  pl.BlockSpec((km, bn), lambda i, j: (0, j)),
                  pl.BlockSpec((kd, bn), lambda i, j: (0, j)),
                  pl.BlockSpec((bm, bn), lambda i, j: (i, j)),
                  pl.BlockSpec((bm, bn), lambda i, j: (i, j + nj))],
        out_specs=pl.BlockSpec((bm, bn), lambda i, j: (i, j)),
        out_shape=jax.ShapeDtypeStruct((m, n), BF16),
        compiler_params=_cparams(("parallel", "parallel")),
        name="gated_merge",
    )(o_mla, o_diff, w_mla, w_diff, gates, gates)


def _mla_prep_body(lat_ref, gq_ref, gkv_ref, wq_ref, wkv_ref, cos_ref, sin_ref,
                   q_ref, k_ref, v_ref, *, q_rank, kv_rank, scale):
    cos = cos_ref[...]
    sin = sin_ref[...]
    cq = lat_ref[:, :q_rank].astype(F32)
    ckv = lat_ref[:, q_rank:q_rank + kv_rank].astype(F32)
    kr = lat_ref[:, q_rank + kv_rank:q_rank + kv_rank + LANE].astype(F32)
    kr_rot = lat_ref[:, q_rank + kv_rank + LANE:q_rank + kv_rank + 2 * LANE].astype(F32)
    k_rope = (kr * cos + kr_rot * sin).astype(BF16)

    cqn = (_rms(cq, NORM_EPS) * gq_ref[...]).astype(BF16)
    qa = jnp.dot(cqn, wq_ref[...], preferred_element_type=F32)
    ckvn = (_rms(ckv, NORM_EPS) * gkv_ref[...]).astype(BF16)
    kv = jnp.dot(ckvn, wkv_ref[...], preferred_element_type=F32)

    hw = MLA_NOPE_DIM + 2 * LANE
    for h in range(MLA_HEADS):
        base = h * hw
        nope = qa[:, base:base + MLA_NOPE_DIM]
        rp = qa[:, base + MLA_NOPE_DIM:base + MLA_NOPE_DIM + LANE]
        rr = qa[:, base + MLA_NOPE_DIM + LANE:base + hw]
        q_ref[:, h * 256:h * 256 + 128] = (nope * scale).astype(BF16)
        q_ref[:, h * 256 + 128:(h + 1) * 256] = ((rp * cos + rr * sin) * scale).astype(BF16)
        k_ref[:, h * 256:h * 256 + 128] = kv[:, h * 128:(h + 1) * 128].astype(BF16)
        k_ref[:, h * 256 + 128:(h + 1) * 256] = k_rope
    v_ref[...] = kv[:, MLA_HEADS * MLA_NOPE_DIM:].astype(BF16)


def mla_prep(lat, gq, gkv, wq, wkv, cos, sin):
    s, lw = lat.shape
    q_rank, kv_rank = gq.shape[0], gkv.shape[0]
    ts = _pick(s, 256, 8)
    hq = MLA_HEADS * 256
    hv = MLA_HEADS * MLA_V_DIM
    row = lambda i: (i, 0)
    fixed = lambda i: (0, 0)
    return pl.pallas_call(
        functools.partial(_mla_prep_body, q_rank=q_rank, kv_rank=kv_rank, scale=LOG2E * MLA_QK_DIM ** -0.5),
        grid=(s // ts,),
        in_specs=[pl.BlockSpec((ts, lw), row),
                  pl.BlockSpec((1, q_rank), fixed), pl.BlockSpec((1, kv_rank), fixed),
                  pl.BlockSpec(wq.shape, fixed), pl.BlockSpec(wkv.shape, fixed),
                  pl.BlockSpec((ts, LANE), row), pl.BlockSpec((ts, LANE), row)],
        out_specs=[pl.BlockSpec((ts, hq), row), pl.BlockSpec((ts, hq), row), pl.BlockSpec((ts, hv), row)],
        out_shape=[jax.ShapeDtypeStruct((s, hq), BF16), jax.ShapeDtypeStruct((s, hq), BF16),
                   jax.ShapeDtypeStruct((s, hv), BF16)],
        compiler_params=_cparams(("parallel",)),
        name="mla_prep",
    )(lat, gq.reshape(1, -1), gkv.reshape(1, -1), wq, wkv, cos, sin)


CHAIN_W = 256
SUM_ROWS = 16


def _softmax_stage(s_ref, p_ref, slot, n, cs, ms, bias=None):
    m_new, alpha = [], []
    for j in range(n):
        s = s_ref[slot, j]
        if bias is not None:
            s = s + bias(j)
        t = jnp.max(s, axis=0, keepdims=True)
        if cs is not None:
            t = t + cs
        m = jnp.maximum(ms[j], t)
        shift = m if cs is None else m - cs
        p_ref[slot, j] = jnp.exp2(s - shift).astype(BF16)
        alpha.append(jnp.exp2(ms[j] - m))
        m_new.append(m)
    return tuple(m_new), tuple(alpha)


def _value_stage(acc_ref, p_ref, slot, n, alpha, vt):
    vt = jnp.concatenate([vt, jnp.ones((SUM_ROWS, vt.shape[1]), vt.dtype)], axis=0)
    for j in range(n):
        acc_ref[j] = alpha[j] * acc_ref[j] + jnp.dot(vt, p_ref[slot, j], preferred_element_type=F32)


def _mla_attn_body(qT_ref, k_ref, vT_ref, o_ref, acc_ref, s_ref, p_ref, *, tk, nk, nc):
    acc_ref[...] = jnp.zeros_like(acc_ref)

    def score_stage(i, slot):
        kt = k_ref[pl.ds(pl.multiple_of(i * tk, tk), tk), :]
        for c in range(nc):
            s_ref[slot, c] = jnp.dot(kt, qT_ref[:, c * CHAIN_W:(c + 1) * CHAIN_W], preferred_element_type=F32)

    def value_stage(i, slot, alpha):
        vt = vT_ref[:, pl.ds(pl.multiple_of(i * tk, tk), tk)]
        _value_stage(acc_ref, p_ref, slot, nc, alpha, vt)

    score_stage(0, 0)

    def body(i2, ms):
        i = 2 * i2
        score_stage(i + 1, 1)
        ms, alpha = _softmax_stage(s_ref, p_ref, 0, nc, None, ms)
        value_stage(i, 0, alpha)
        score_stage(jnp.minimum(i + 2, nk - 1), 0)
        ms, alpha = _softmax_stage(s_ref, p_ref, 1, nc, None, ms)
        value_stage(i + 1, 1, alpha)
        return ms

    neg = jnp.full((1, CHAIN_W), NEG_BIG, F32)
    lax.fori_loop(0, nk // 2, body, (neg,) * nc)
    for c in range(nc):
        o = acc_ref[c, :MLA_V_DIM] / acc_ref[c, MLA_V_DIM:MLA_V_DIM + 1]
        o_ref[:, c * CHAIN_W:(c + 1) * CHAIN_W] = o.astype(o_ref.dtype)


def mla_attention(qT, k, vT, nc=8, tk=1024):
    s = k.shape[0]
    tq = _pick(s, nc * CHAIN_W, CHAIN_W)
    nc = tq // CHAIN_W
    tk = _pick(s // 2, tk)
    nk = s // tk
    assert nk % 2 == 0
    dva = MLA_V_DIM + SUM_ROWS
    return pl.pallas_call(
        functools.partial(_mla_attn_body, tk=tk, nk=nk, nc=nc),
        grid=(MLA_HEADS, s // tq),
        in_specs=[pl.BlockSpec((256, tq), lambda h, i: (h, i)),
                  pl.BlockSpec((s, 256), lambda h, i: (0, h)),
                  pl.BlockSpec((MLA_V_DIM, s), lambda h, i: (h, 0))],
        out_specs=pl.BlockSpec((MLA_V_DIM, tq), lambda h, i: (h, i)),
        out_shape=jax.ShapeDtypeStruct((MLA_HEADS * MLA_V_DIM, s), BF16),
        scratch_shapes=[pltpu.VMEM((nc, dva, CHAIN_W), F32),
                        pltpu.VMEM((2, nc, tk, CHAIN_W), F32),
                        pltpu.VMEM((2, nc, tk, CHAIN_W), BF16)],
        compiler_params=_cparams(("parallel", "parallel")),
        name="mla_attention",
    )(qT, k, vT)


T5_BUCKET_EDGES = (0, 1, 2, 3, 4, 5, 6, 7, 8, 12, 16, 23, 32, 46, 64, 91)


def _t5_bias(rel, tab_ref, h):
    nb = REL_BUCKETS // 2
    pos = rel > 0
    n = jnp.abs(rel)
    v = jnp.where(pos, tab_ref[h, nb], tab_ref[h, 0])
    for b in range(1, nb):
        leaf = jnp.where(pos, tab_ref[h, nb + b], tab_ref[h, b])
        v = jnp.where(n >= T5_BUCKET_EDGES[b], leaf, v)
    return v


def _diff_attn_body(order_ref, nfar_ref, kmin_ref, kmax_ref, smin_ref, smax_ref, qmin_ref, qmax_ref,
                    cmin_ref, cmax_ref,
                    tab_ref, lam_ref, qT_ref, k_ref, vT_ref, posq_ref, posk_ref, g_ref,
                    o_ref, qm_ref, acc_ref, s_ref, p_ref, bias_ref, *, tk, nk, nc, sk, out_scale):
    h = pl.program_id(0)
    qi = pl.program_id(1)
    nch = 2 * nc
    qT = qT_ref[...]
    row = lax.broadcasted_iota(jnp.int32, qT.shape, 0)
    zero = jnp.zeros_like(qT)
    qm_ref[0] = jnp.where(row < DIFF_HEAD_DIM, qT, zero)
    qm_ref[1] = jnp.where(row >= DIFF_HEAD_DIM, qT, zero)
    b_after = tab_ref[h, REL_BUCKETS - 1]
    b_before = tab_ref[h, REL_BUCKETS // 2 - 1]
    q_lo = qmin_ref[qi]
    q_hi = qmax_ref[qi]
    acc_ref[...] = jnp.zeros_like(acc_ref)
    nsub = tk // sk
    nfar = nfar_ref[qi]

    def tile_at(t):
        return order_ref[qi * nk + t]

    def score_stage(i, slot):
        kt = k_ref[pl.ds(pl.multiple_of(i * tk, tk), tk), :]
        for j in range(nch):
            q = qm_ref[j % 2, :, (j // 2) * CHAIN_W:(j // 2 + 1) * CHAIN_W]
            s_ref[slot, j] = jnp.dot(kt, q, preferred_element_type=F32)

    def value_stage(i, slot, alpha):
        vt = vT_ref[:, pl.ds(pl.multiple_of(i * tk, tk), tk)]
        _value_stage(acc_ref, p_ref, slot, nch, alpha, vt)

    def far_bias(i):
        return jnp.where(kmin_ref[i] - q_hi >= REL_MAX_DISTANCE, b_after, b_before)

    @pl.when(nfar > 0)
    def _():
        score_stage(tile_at(0), 0)

    def far_body(i2, ms):
        t = 2 * i2
        i0, i1 = tile_at(t), tile_at(t + 1)
        score_stage(i1, 1)
        ms, alpha = _softmax_stage(s_ref, p_ref, 0, nch, far_bias(i0), ms)
        value_stage(i0, 0, alpha)
        score_stage(tile_at(jnp.minimum(t + 2, nfar - 1)), 0)
        ms, alpha = _softmax_stage(s_ref, p_ref, 1, nch, far_bias(i1), ms)
        value_stage(i1, 1, alpha)
        return ms

    neg = jnp.full((1, CHAIN_W), NEG_BIG, F32)
    ms = lax.fori_loop(0, nfar // 2, far_body, (neg,) * nch)

    def near_body(t, ms):
        i = tile_at(t)
        off = pl.multiple_of(i * tk, tk)

        def fill(n, _):
            tt, cc = n // nc, n % nc
            u = i * nsub + tt
            cq = qi * nc + cc
            blk_after = smin_ref[u] - cmax_ref[cq] >= REL_MAX_DISTANCE
            blk_before = smax_ref[u] - cmin_ref[cq] <= -REL_MAX_DISTANCE
            roff = pl.multiple_of(tt * sk, sk)
            coff = pl.multiple_of(cc * CHAIN_W, CHAIN_W)

            @pl.when(jnp.logical_or(blk_after, blk_before))
            def _():
                c = jnp.where(blk_after, b_after, b_before)
                bias_ref[pl.ds(roff, sk), pl.ds(coff, CHAIN_W)] = jnp.full((sk, CHAIN_W), c, F32)

            @pl.when(jnp.logical_not(jnp.logical_or(blk_after, blk_before)))
            def _():
                pk = posk_ref[pl.ds(pl.multiple_of(off + roff, sk), sk), :]
                pk = jnp.concatenate([pk] * (CHAIN_W // LANE), axis=1)
                rel = pk - posq_ref[:, pl.ds(coff, CHAIN_W)]
                bias_ref[pl.ds(roff, sk), pl.ds(coff, CHAIN_W)] = _t5_bias(rel, tab_ref, h)

            return 0

        lax.fori_loop(0, nsub * nc, fill, 0)
        score_stage(i, 0)
        ms, alpha = _softmax_stage(s_ref, p_ref, 0, nch, None, ms,
                                   bias=lambda j: bias_ref[:, (j // 2) * CHAIN_W:(j // 2 + 1) * CHAIN_W])
        value_stage(i, 0, alpha)
        return ms

    lax.fori_loop(nfar, nk, near_body, ms)

    lam = lam_ref[0]
    dv = DIFF_V_DIM
    for c in range(nc):
        o1 = acc_ref[2 * c, :dv] / acc_ref[2 * c, dv:dv + 1]
        o2 = acc_ref[2 * c + 1, :dv] / acc_ref[2 * c + 1, dv:dv + 1]
        o = o1 - lam * o2
        msq = jnp.mean(o * o, axis=0, keepdims=True)
        o = o * lax.rsqrt(msq + DIFF_SUBLN_EPS) * g_ref[...]
        o_ref[:, c * CHAIN_W:(c + 1) * CHAIN_W] = (o * out_scale).astype(o_ref.dtype)


def diff_attention(qT, qkv, vT, positions, rel_bias, lam, subln, lam_init, nc=4, tk=512, sk=128):
    s = qkv.shape[0]
    tq = _pick(s, nc * CHAIN_W, CHAIN_W)
    nc = tq // CHAIN_W
    tk = _pick(s, tk)
    sk = _pick(tk, sk)
    nk, nq = s // tk, s // tq
    pos = positions.reshape(s).astype(jnp.int32)
    kmin = pos.reshape(nk, tk).min(axis=1)
    kmax = pos.reshape(nk, tk).max(axis=1)
    smin = pos.reshape(s // sk, sk).min(axis=1)
    smax = pos.reshape(s // sk, sk).max(axis=1)
    qmin = pos.reshape(nq, tq).min(axis=1)
    qmax = pos.reshape(nq, tq).max(axis=1)
    cmin = pos.reshape(s // CHAIN_W, CHAIN_W).min(axis=1)
    cmax = pos.reshape(s // CHAIN_W, CHAIN_W).max(axis=1)
    far = jnp.logical_or(kmin[None, :] - qmax[:, None] >= REL_MAX_DISTANCE,
                         kmax[None, :] - qmin[:, None] <= -REL_MAX_DISTANCE)
    order = jnp.argsort(jnp.logical_not(far), axis=1, stable=True).astype(jnp.int32)
    nfar = far.sum(axis=1).astype(jnp.int32)
    nfar = nfar - nfar % 2
    posq = pos.reshape(1, s)
    posk = jnp.broadcast_to(pos.reshape(s, 1), (s, LANE))
    tab = (rel_bias.T * LOG2E).astype(F32)
    dva = DIFF_V_DIM + SUM_ROWS
    smem = pl.BlockSpec(memory_space=pltpu.SMEM)
    grid_spec = pltpu.PrefetchScalarGridSpec(
        num_scalar_prefetch=10,
        grid=(DIFF_HEADS, nq),
        in_specs=[smem, smem,
                  pl.BlockSpec((DIFF_V_DIM, tq), lambda h, i, *_: (h, i)),
                  pl.BlockSpec((s, 2 * DIFF_HEAD_DIM), lambda h, i, *_: (0, DIFF_HEADS + h)),
                  pl.BlockSpec((DIFF_V_DIM, s), lambda h, i, *_: (h, 0)),
                  pl.BlockSpec((1, tq), lambda h, i, *_: (0, i)),
                  pl.BlockSpec((s, LANE), lambda h, i, *_: (0, 0)),
                  pl.BlockSpec((DIFF_V_DIM, 1), lambda h, i, *_: (0, 0))],
        out_specs=pl.BlockSpec((DIFF_V_DIM, tq), lambda h, i, *_: (h, i)),
        scratch_shapes=[pltpu.VMEM((2, DIFF_V_DIM, tq), BF16),
                        pltpu.VMEM((2 * nc, dva, CHAIN_W), F32),
                        pltpu.VMEM((2, 2 * nc, tk, CHAIN_W), F32),
                        pltpu.VMEM((2, 2 * nc, tk, CHAIN_W), BF16),
                        pltpu.VMEM((tk, tq), F32)],
    )
    return pl.pallas_call(
        functools.partial(_diff_attn_body, tk=tk, nk=nk, nc=nc, sk=sk, out_scale=1.0 - lam_init),
        grid_spec=grid_spec,
        out_shape=jax.ShapeDtypeStruct((DIFF_HEADS * DIFF_V_DIM, s), BF16),
        compiler_params=_cparams(("parallel", "parallel")),
        name="diff_attention",
    )(order.reshape(nq * nk), nfar, kmin, kmax, smin, smax, qmin, qmax, cmin, cmax,
      tab, lam.reshape(1).astype(F32),
      qT, qkv, vT, posq, posk,
      subln.reshape(DIFF_V_DIM, 1).astype(F32))


def _xa_body(x_ref, gpre_ref, wq_ref, kT_ref, v_ref, wo_ref, gpost_ref, o_ref):
    x = x_ref[...]
    h = (_rms(x, NORM_EPS) * gpre_ref[...]).astype(BF16)
    q = jnp.dot(h, wq_ref[...], preferred_element_type=F32) * (MEM_HEAD_DIM ** -0.5)
    q = q.astype(BF16)
    outs = []
    for hd in range(MEM_HEADS):
        sl = slice(hd * MEM_HEAD_DIM, (hd + 1) * MEM_HEAD_DIM)
        s = jnp.dot(q[:, sl], kT_ref[sl, :], preferred_element_type=F32)
        p = jnp.exp(s - jnp.max(s, axis=-1, keepdims=True))
        l = jnp.sum(p, axis=-1, keepdims=True)
        o = jnp.dot(p.astype(BF16), v_ref[:, sl], preferred_element_type=F32) / l
        outs.append(o.astype(BF16))
    o = jnp.concatenate(outs, axis=1)
    y = jnp.dot(o, wo_ref[...], preferred_element_type=F32)
    o_ref[...] = x + _rms(y, NORM_EPS) * gpost_ref[...]


def cross_attention_sublayer(x, g_pre, w_q, kT, v, w_o, g_post):
    s, d = x.shape
    ts = _pick(s, 256, 8)
    row = lambda i: (i, 0)
    fixed = lambda i: (0, 0)
    return pl.pallas_call(
        _xa_body,
        grid=(s // ts,),
        in_specs=[pl.BlockSpec((ts, d), row), pl.BlockSpec((1, d), fixed),
                  pl.BlockSpec(w_q.shape, fixed), pl.BlockSpec(kT.shape, fixed),
                  pl.BlockSpec(v.shape, fixed), pl.BlockSpec(w_o.shape, fixed),
                  pl.BlockSpec((1, d), fixed)],
        out_specs=pl.BlockSpec((ts, d), row),
        out_shape=jax.ShapeDtypeStruct((s, d), F32),
        compiler_params=_cparams(("parallel",)),
        name="cross_attention",
    )(x, g_pre.reshape(1, d), w_q, kT, v, w_o, g_post.reshape(1, d))


def _rot_cols(w):
    half = w.shape[-1] // 2
    return jnp.concatenate([-w[..., half:], w[..., :half]], axis=-1)


def _prep_w_in(w, q_rank, kv_rank):
    a = q_rank + kv_rank
    kr = w[:, a:a + MLA_ROPE_DIM]
    z = jnp.zeros_like(kr)
    w_lat = jnp.concatenate([w[:, :a], kr, z, _rot_cols(kr), z], axis=1)
    b = a + MLA_ROPE_DIM
    nq = DIFF_HEADS * DIFF_V_DIM
    w_q = w[:, b:b + nq] * (LOG2E * DIFF_HEAD_DIM ** -0.5)
    w_qkvd = jnp.concatenate([w_q, w[:, b + nq:b + 3 * nq]], axis=1)
    return w_lat.astype(BF16), w_qkvd.astype(BF16), w[:, b + 3 * nq:].astype(BF16)


def _prep_w_uq(w):
    r = w.shape[0]
    w3 = w.reshape(r, MLA_HEADS, MLA_QK_DIM)
    nope, rope = w3[..., :MLA_NOPE_DIM], w3[..., MLA_NOPE_DIM:]
    z = jnp.zeros_like(rope)
    out = jnp.concatenate([nope, rope, z, _rot_cols(rope), z], axis=-1)
    return out.reshape(r, -1).astype(BF16)


def _prep_w_ukv(w):
    r = w.shape[0]
    w3 = w.reshape(r, MLA_HEADS, MLA_NOPE_DIM + MLA_V_DIM)
    k_nope = w3[..., :MLA_NOPE_DIM].reshape(r, -1)
    v = w3[..., MLA_NOPE_DIM:].reshape(r, -1)
    return jnp.concatenate([k_nope, v], axis=1).astype(BF16)


def _rope_tables(positions, s):
    half = MLA_ROPE_DIM // 2
    inv_freq = ROPE_BASE ** (-jnp.arange(half, dtype=F32) / half)
    ang = positions.reshape(s, 1).astype(F32) * inv_freq
    z = jnp.zeros((s, LANE - MLA_ROPE_DIM), F32)
    cos = jnp.concatenate([jnp.cos(ang), jnp.cos(ang), z], axis=1)
    sin = jnp.concatenate([jnp.sin(ang), jnp.sin(ang), z], axis=1)
    return cos, sin


def kernel(x, mem, positions, rel_bias, mix_norm_pre, mix_norm_post, w_in, mla_q_norm, mla_w_uq, mla_kv_norm, mla_w_ukv, diff_lambda, diff_subln, w_mla_branch, w_diff_branch, w_out, xa_norm_pre, xa_norm_post, xa_mem_norm, xa_w_q, xa_w_kv, xa_w_o, ffn_norm_pre, ffn_norm_post, ffn_w_in, ffn_w_out):
    b, s, d = x.shape
    assert b == 1, "kernel is written for batch 1"
    depth = w_in.shape[0]
    q_rank, kv_rank = mla_q_norm.shape[1], mla_kv_norm.shape[1]
    dff = ffn_w_out.shape[1]
    dff_pad = -(-dff // 512) * 512
    nh = DIFF_HEADS * DIFF_V_DIM

    xs = x.reshape(s, d)
    mems = mem.reshape(mem.shape[1], d)
    cos, sin = _rope_tables(positions, s)

    for l in range(depth):
        w_lat, w_qkvd, w_gate = _prep_w_in(w_in[l], q_rank, kv_rank)
        h = rmsnorm(xs, mix_norm_pre[l], BF16)
        lat = matmul(h, w_lat, F32, bn=768)
        qkvd = matmul(h, w_qkvd, BF16)
        gates = matmul(h, w_gate, BF16)

        q, k, v = mla_prep(lat, mla_q_norm[l], mla_kv_norm[l], _prep_w_uq(mla_w_uq[l]),
                           _prep_w_ukv(mla_w_ukv[l]), cos, sin)
        o_mla = mla_attention(q.T, k, v.T).T

        lam_init = 0.8 - 0.6 * math.exp(-0.3 * l)
        lv = diff_lambda[l].astype(F32)
        lam = jnp.exp(jnp.sum(lv[0] * lv[1])) - jnp.exp(jnp.sum(lv[2] * lv[3])) + lam_init
        qT_d = qkvd[:, :nh].T
        vT_d = qkvd[:, 2 * nh:].T
        o_diff = diff_attention(qT_d, qkvd, vT_d, positions, rel_bias, lam, diff_subln[l], lam_init).T

        merged = gated_merge(o_mla, o_diff, w_mla_branch[l].astype(BF16), w_diff_branch[l].astype(BF16), gates)
        y = matmul(merged, w_out[l].astype(BF16), F32)
        xs = add_rmsnorm(xs, y, mix_norm_post[l])

        mem_n = rmsnorm(mems, xa_mem_norm[l], BF16)
        kv = matmul(mem_n, xa_w_kv[l].astype(BF16), BF16)
        hm = MEM_HEADS * MEM_HEAD_DIM
        xs = cross_attention_sublayer(xs, xa_norm_pre[l], xa_w_q[l].astype(BF16), kv[:, :hm].T, kv[:, hm:],
                                      xa_w_o[l].astype(BF16), xa_norm_post[l])

        wi = ffn_w_in[l]
        pad = dff_pad - dff
        w_gu = jnp.concatenate([jnp.pad(wi[:, :dff], ((0, 0), (0, pad))),
                                jnp.pad(wi[:, dff:], ((0, 0), (0, pad)))], axis=1).astype(BF16)
        w_fo = jnp.pad(ffn_w_out[l], ((0, pad), (0, 0))).astype(BF16)
        h = rmsnorm(xs, ffn_norm_pre[l], BF16)
        act = swiglu_matmul(h, w_gu, dff_pad)
        y = matmul(act, w_fo, F32, bk=2816)
        xs = add_rmsnorm(xs, y, ffn_norm_post[l])

    return xs.reshape(b, s, d)
```

```python
import functools
import math

import jax
import jax.numpy as jnp
from jax import lax
from jax.experimental import pallas as pl
from jax.experimental.pallas import tpu as pltpu

F32 = jnp.float32
BF16 = jnp.bfloat16

MLA_HEADS = 8
MLA_NOPE_DIM = 128
MLA_ROPE_DIM = 64
MLA_V_DIM = 128
MLA_QK_DIM = MLA_NOPE_DIM + MLA_ROPE_DIM
DIFF_HEADS = 8
DIFF_HEAD_DIM = 64
DIFF_V_DIM = 2 * DIFF_HEAD_DIM
MEM_HEADS = 4
MEM_HEAD_DIM = 128
REL_BUCKETS = 32
REL_MAX_DISTANCE = 128
ROPE_BASE = 10000.0
NORM_EPS = 1e-6
DIFF_SUBLN_EPS = 1e-5

LANE = 128
VMEM_LIMIT_BYTES = 56 * 1024 * 1024
NEG_BIG = -1e30
LOG2E = math.log2(math.e)


def _cparams(sem):
    return pltpu.CompilerParams(dimension_semantics=sem, vmem_limit_bytes=VMEM_LIMIT_BYTES)


def _pick(dim, target, align=LANE):
    if dim <= target:
        return dim
    t = (target // align) * align
    while t >= align:
        if dim % t == 0:
            return t
        t -= align
    return dim


def _sigmoid(x):
    return 1.0 / (1.0 + jnp.exp(-x))


def _rms(x, eps):
    return x * lax.rsqrt(jnp.mean(x * x, axis=-1, keepdims=True) + eps)


def _rmsnorm_body(x_ref, g_ref, o_ref, *, eps):
    x = x_ref[...].astype(F32)
    o_ref[...] = (_rms(x, eps) * g_ref[...]).astype(o_ref.dtype)


def rmsnorm(x, g, out_dtype, eps=NORM_EPS):
    m, d = x.shape
    bm = _pick(m, 256, 8)
    return pl.pallas_call(
        functools.partial(_rmsnorm_body, eps=eps),
        grid=(m // bm,),
        in_specs=[pl.BlockSpec((bm, d), lambda i: (i, 0)), pl.BlockSpec((1, d), lambda i: (0, 0))],
        out_specs=pl.BlockSpec((bm, d), lambda i: (i, 0)),
        out_shape=jax.ShapeDtypeStruct((m, d), out_dtype),
        compiler_params=_cparams(("parallel",)),
        name="rmsnorm",
    )(x, g.reshape(1, d))


def _add_rmsnorm_body(x_ref, y_ref, g_ref, *rest, eps, with_next):
    y = y_ref[...].astype(F32)
    x = x_ref[...] + _rms(y, eps) * g_ref[...]
    if with_next:
        gn_ref, o_ref, h_ref = rest
        h_ref[...] = (_rms(x, eps) * gn_ref[...]).astype(h_ref.dtype)
    else:
        (o_ref,) = rest
    o_ref[...] = x


def add_rmsnorm(x, y, g, g_next=None, eps=NORM_EPS):
    m, d = x.shape
    bm = _pick(m, 256, 8)
    row = pl.BlockSpec((bm, d), lambda i: (i, 0))
    vec = pl.BlockSpec((1, d), lambda i: (0, 0))
    with_next = g_next is not None
    args = (x, y, g.reshape(1, d)) + ((g_next.reshape(1, d),) if with_next else ())
    return pl.pallas_call(
        functools.partial(_add_rmsnorm_body, eps=eps, with_next=with_next),
        grid=(m // bm,),
        in_specs=[row, row, vec] + ([vec] if with_next else []),
        out_specs=[row, row] if with_next else row,
        out_shape=([jax.ShapeDtypeStruct((m, d), F32), jax.ShapeDtypeStruct((m, d), BF16)] if with_next
                   else jax.ShapeDtypeStruct((m, d), F32)),
        compiler_params=_cparams(("parallel",)),
        name="add_rmsnorm",
    )(*args)


def _matmul_body(a_ref, w_ref, o_ref, *scratch, nk):
    part = jnp.dot(a_ref[...], w_ref[...], preferred_element_type=F32)
    if nk == 1:
        o_ref[...] = part.astype(o_ref.dtype)
        return
    (acc_ref,) = scratch
    k = pl.program_id(2)

    @pl.when(k == 0)
    def _():
        acc_ref[...] = part

    @pl.when(k > 0)
    def _():
        acc_ref[...] += part

    @pl.when(k == nk - 1)
    def _():
        o_ref[...] = acc_ref[...].astype(o_ref.dtype)


def matmul(a, w, l, out_dtype, bm=1024, bn=1024, bk=4096):
    m, kd = a.shape
    _, _, n = w.shape
    bm, bn, bk = _pick(m, bm, 8), _pick(n, bn), _pick(kd, bk)
    nk = kd // bk
    return pl.pallas_call(
        functools.partial(_matmul_body, nk=nk),
        grid=(m // bm, n // bn, nk),
        in_specs=[pl.BlockSpec((bm, bk), lambda i, j, k: (i, k)),
                  pl.BlockSpec((None, bk, bn), lambda i, j, k: (l, k, j))],
        out_specs=pl.BlockSpec((bm, bn), lambda i, j, k: (i, j)),
        out_shape=jax.ShapeDtypeStruct((m, n), out_dtype),
        scratch_shapes=[pltpu.VMEM((bm, bn), F32)] if nk > 1 else [],
        compiler_params=_cparams(("parallel", "parallel", "arbitrary")),
        name="matmul",
    )(a, w)


def _swiglu_body(a_ref, wg_ref, wu_ref, o_ref):
    a = a_ref[...]
    g = jnp.dot(a, wg_ref[...], preferred_element_type=F32)
    u = jnp.dot(a, wu_ref[...], preferred_element_type=F32)
    o_ref[...] = (g * _sigmoid(g) * u).astype(o_ref.dtype)


def swiglu_matmul(a, w_gu, l, dff, bm=1024, bn=512):
    m, kd = a.shape
    bm, bn = _pick(m, bm, 8), _pick(dff, bn)
    nj = dff // bn
    return pl.pallas_call(
        _swiglu_body,
        grid=(m // bm, nj),
        in_specs=[pl.BlockSpec((bm, kd), lambda i, j: (i, 0)),
                  pl.BlockSpec((None, kd, bn), lambda i, j: (l, 0, j)),
                  pl.BlockSpec((None, kd, bn), lambda i, j: (l, 0, j + nj))],
        out_specs=pl.BlockSpec((bm, bn), lambda i, j: (i, j)),
        out_shape=jax.ShapeDtypeStruct((m, dff), BF16),
        compiler_params=_cparams(("parallel", "parallel")),
        name="swiglu_matmul",
    )(a, w_gu, w_gu)


def _merge_body(om_ref, od_ref, wm_ref, wd_ref, gm_ref, gd_ref, o_ref):
    ym = jnp.dot(om_ref[...], wm_ref[...], preferred_element_type=F32)
    yd = jnp.dot(od_ref[...], wd_ref[...], preferred_element_type=F32)
    sm = _sigmoid(gm_ref[...].astype(F32))
    sd = _sigmoid(gd_ref[...].astype(F32))
    o_ref[...] = (sm * ym + sd * yd).astype(o_ref.dtype)


def gated_merge(o_mla, o_diff, w_mla, w_diff, l, gates, bm=1024, bn=1024):
    m, km = o_mla.shape
    _, kd = o_diff.shape
    _, _, n = w_mla.shape
    bm, bn = _pick(m, bm, 8), _pick(n, bn)
    nj = n // bn
    return pl.pallas_call(
        _merge_body,
        grid=(m // bm, nj),
        in_specs=[pl.BlockSpec((bm, km), lambda i, j: (i, 0)),
                  pl.BlockSpec((bm, kd), lambda i, j: (i, 0)),
                  pl.BlockSpec((None, km, bn), lambda i, j: (l, 0, j)),
                  pl.BlockSpec((None, kd, bn), lambda i, j: (l, 0, j)),
                  pl.BlockSpec((bm, bn), lambda i, j: (i, j)),
                  pl.BlockSpec((bm, bn), lambda i, j: (i, j + nj))],
        out_specs=pl.BlockSpec((bm, bn), lambda i, j: (i, j)),
        out_shape=jax.ShapeDtypeStruct((m, n), BF16),
        compiler_params=_cparams(("parallel", "parallel")),
        name="gated_merge",
    )(o_mla, o_diff, w_mla, w_diff, gates, gates)


def _mla_prep_body(lat_ref, gq_ref, gkv_ref, wq_ref, wkv_ref, cos_ref, sin_ref,
                   q_ref, k_ref, v_ref, *, q_rank, kv_rank, scale):
    cos = cos_ref[...]
    sin = sin_ref[...]
    cq = lat_ref[:, :q_rank].astype(F32)
    ckv = lat_ref[:, q_rank:q_rank + kv_rank].astype(F32)
    kr = lat_ref[:, q_rank + kv_rank:q_rank + kv_rank + LANE].astype(F32)
    kr_rot = lat_ref[:, q_rank + kv_rank + LANE:q_rank + kv_rank + 2 * LANE].astype(F32)
    k_rope = (kr * cos + kr_rot * sin).astype(BF16)

    cqn = (_rms(cq, NORM_EPS) * gq_ref[...]).astype(BF16)
    qa = jnp.dot(cqn, wq_ref[...], preferred_element_type=F32)
    ckvn = (_rms(ckv, NORM_EPS) * gkv_ref[...]).astype(BF16)
    kv = jnp.dot(ckvn, wkv_ref[...], preferred_element_type=F32)

    hw = MLA_NOPE_DIM + 2 * LANE
    for h in range(MLA_HEADS):
        base = h * hw
        nope = qa[:, base:base + MLA_NOPE_DIM]
        rp = qa[:, base + MLA_NOPE_DIM:base + MLA_NOPE_DIM + LANE]
        rr = qa[:, base + MLA_NOPE_DIM + LANE:base + hw]
        q_ref[:, h * 256:h * 256 + 128] = (nope * scale).astype(BF16)
        q_ref[:, h * 256 + 128:(h + 1) * 256] = ((rp * cos + rr * sin) * scale).astype(BF16)
        k_ref[:, h * 256:h * 256 + 128] = kv[:, h * 128:(h + 1) * 128].astype(BF16)
        k_ref[:, h * 256 + 128:(h + 1) * 256] = k_rope
    v_ref[...] = kv[:, MLA_HEADS * MLA_NOPE_DIM:].astype(BF16)


def mla_prep(lat, gq, gkv, wq, wkv, l, cos, sin):
    s, lw = lat.shape
    q_rank, kv_rank = gq.shape[0], gkv.shape[0]
    ts = _pick(s, 256, 8)
    hq = MLA_HEADS * 256
    hv = MLA_HEADS * MLA_V_DIM
    row = lambda i: (i, 0)
    fixed = lambda i: (0, 0)
    return pl.pallas_call(
        functools.partial(_mla_prep_body, q_rank=q_rank, kv_rank=kv_rank, scale=LOG2E * MLA_QK_DIM ** -0.5),
        grid=(s // ts,),
        in_specs=[pl.BlockSpec((ts, lw), row),
                  pl.BlockSpec((1, q_rank), fixed), pl.BlockSpec((1, kv_rank), fixed),
                  pl.BlockSpec((None,) + wq.shape[1:], lambda i: (l, 0, 0)),
                  pl.BlockSpec((None,) + wkv.shape[1:], lambda i: (l, 0, 0)),
                  pl.BlockSpec((ts, LANE), row), pl.BlockSpec((ts, LANE), row)],
        out_specs=[pl.BlockSpec((ts, hq), row), pl.BlockSpec((ts, hq), row), pl.BlockSpec((ts, hv), row)],
        out_shape=[jax.ShapeDtypeStruct((s, hq), BF16), jax.ShapeDtypeStruct((s, hq), BF16),
                   jax.ShapeDtypeStruct((s, hv), BF16)],
        compiler_params=_cparams(("parallel",)),
        name="mla_prep",
    )(lat, gq.reshape(1, -1), gkv.reshape(1, -1), wq, wkv, cos, sin)


CHAIN_W = 256
SUM_ROWS = 16


def _softmax_stage(s_ref, p_ref, slot, n, cs, ms, bias=None):
    m_new, alpha = [], []
    for j in range(n):
        s = s_ref[slot, j]
        if bias is not None:
            s = s + bias(j)
        t = jnp.max(s, axis=0, keepdims=True)
        if cs is None:
            shift = m = jnp.maximum(ms[j], t)
        else:
            shift = jnp.maximum(ms[j] - cs, t)
            m = shift + cs
        p_ref[slot, j] = jnp.exp2(s - shift).astype(BF16)
        alpha.append(jnp.exp2(ms[j] - m))
        m_new.append(m)
    return tuple(m_new), tuple(alpha)


def _value_stage(acc_ref, p_ref, slot, n, alpha, vt):
    vt = jnp.concatenate([vt, jnp.ones((SUM_ROWS, vt.shape[1]), vt.dtype)], axis=0)
    for j in range(n):
        acc_ref[j] = alpha[j] * acc_ref[j] + jnp.dot(vt, p_ref[slot, j], preferred_element_type=F32)


def _mla_attn_body(qT_ref, k_ref, vT_ref, o_ref, acc_ref, s_ref, p_ref, *, tk, nk, nc):
    acc_ref[...] = jnp.zeros_like(acc_ref)

    def score_stage(i, slot):
        kt = k_ref[pl.ds(pl.multiple_of(i * tk, tk), tk), :]
        for c in range(nc):
            s_ref[slot, c] = jnp.dot(kt, qT_ref[:, c * CHAIN_W:(c + 1) * CHAIN_W], preferred_element_type=F32)

    def value_stage(i, slot, alpha):
        vt = vT_ref[:, pl.ds(pl.multiple_of(i * tk, tk), tk)]
        _value_stage(acc_ref, p_ref, slot, nc, alpha, vt)

    score_stage(0, 0)

    def body(i2, ms):
        i = 2 * i2
        score_stage(i + 1, 1)
        ms, alpha = _softmax_stage(s_ref, p_ref, 0, nc, None, ms)
        value_stage(i, 0, alpha)
        score_stage(jnp.minimum(i + 2, nk - 1), 0)
        ms, alpha = _softmax_stage(s_ref, p_ref, 1, nc, None, ms)
        value_stage(i + 1, 1, alpha)
        return ms

    neg = jnp.full((1, CHAIN_W), NEG_BIG, F32)
    lax.fori_loop(0, nk // 2, body, (neg,) * nc)
    for c in range(nc):
        o = acc_ref[c, :MLA_V_DIM] / acc_ref[c, MLA_V_DIM:MLA_V_DIM + 1]
        o_ref[:, c * CHAIN_W:(c + 1) * CHAIN_W] = o.astype(o_ref.dtype)


def mla_attention(qT, k, vT, nc=8, tk=1024):
    s = k.shape[0]
    tq = _pick(s, nc * CHAIN_W, CHAIN_W)
    nc = tq // CHAIN_W
    tk = _pick(s // 2, tk)
    nk = s // tk
    assert nk % 2 == 0
    dva = MLA_V_DIM + SUM_ROWS
    return pl.pallas_call(
        functools.partial(_mla_attn_body, tk=tk, nk=nk, nc=nc),
        grid=(MLA_HEADS, s // tq),
        in_specs=[pl.BlockSpec((256, tq), lambda h, i: (h, i)),
                  pl.BlockSpec((s, 256), lambda h, i: (0, h)),
                  pl.BlockSpec((MLA_V_DIM, s), lambda h, i: (h, 0))],
        out_specs=pl.BlockSpec((MLA_V_DIM, tq), lambda h, i: (h, i)),
        out_shape=jax.ShapeDtypeStruct((MLA_HEADS * MLA_V_DIM, s), BF16),
        scratch_shapes=[pltpu.VMEM((nc, dva, CHAIN_W), F32),
                        pltpu.VMEM((2, nc, tk, CHAIN_W), F32),
                        pltpu.VMEM((2, nc, tk, CHAIN_W), BF16)],
        compiler_params=_cparams(("parallel", "parallel")),
        name="mla_attention",
    )(qT, k, vT)


T5_BUCKET_EDGES = (0, 1, 2, 3, 4, 5, 6, 7, 8, 12, 16, 23, 32, 46, 64, 91)


def _t5_bias(rel, tab_ref, h):
    nb = REL_BUCKETS // 2
    pos = rel > 0
    n = jnp.abs(rel)
    v = jnp.where(pos, tab_ref[h, nb], tab_ref[h, 0])
    for b in range(1, nb):
        leaf = jnp.where(pos, tab_ref[h, nb + b], tab_ref[h, b])
        v = jnp.where(n >= T5_BUCKET_EDGES[b], leaf, v)
    return v


def _diff_attn_body(order_ref, nfar_ref, kmin_ref, kmax_ref, smin_ref, smax_ref, qmin_ref, qmax_ref,
                    cmin_ref, cmax_ref,
                    tab_ref, lam_ref, qT_ref, k_ref, vT_ref, posq_ref, posk_ref, g_ref,
                    o_ref, qm_ref, acc_ref, s_ref, p_ref, bias_ref, *, tk, nk, nc, sk, out_scale):
    h = pl.program_id(0)
    qi = pl.program_id(1)
    nch = 2 * nc
    qT = qT_ref[...]
    row = lax.broadcasted_iota(jnp.int32, qT.shape, 0)
    zero = jnp.zeros_like(qT)
    qm_ref[0] = jnp.where(row < DIFF_HEAD_DIM, qT, zero)
    qm_ref[1] = jnp.where(row >= DIFF_HEAD_DIM, qT, zero)
    b_after = tab_ref[h, REL_BUCKETS - 1]
    b_before = tab_ref[h, REL_BUCKETS // 2 - 1]
    q_lo = qmin_ref[qi]
    q_hi = qmax_ref[qi]
    acc_ref[...] = jnp.zeros_like(acc_ref)
    nsub = tk // sk
    nfar = nfar_ref[qi]

    def tile_at(t):
        return order_ref[qi * nk + t]

    def score_stage(i, slot):
        kt = k_ref[pl.ds(pl.multiple_of(i * tk, tk), tk), :]
        for j in range(nch):
            q = qm_ref[j % 2, :, (j // 2) * CHAIN_W:(j // 2 + 1) * CHAIN_W]
            s_ref[slot, j] = jnp.dot(kt, q, preferred_element_type=F32)

    def value_stage(i, slot, alpha):
        vt = vT_ref[:, pl.ds(pl.multiple_of(i * tk, tk), tk)]
        _value_stage(acc_ref, p_ref, slot, nch, alpha, vt)

    def far_bias(i):
        return jnp.where(kmin_ref[i] - q_hi >= REL_MAX_DISTANCE, b_after, b_before)

    @pl.when(nfar > 0)
    def _():
        score_stage(tile_at(0), 0)

    def far_body(i2, ms):
        t = 2 * i2
        i0, i1 = tile_at(t), tile_at(t + 1)
        score_stage(i1, 1)
        ms, alpha = _softmax_stage(s_ref, p_ref, 0, nch, far_bias(i0), ms)
        value_stage(i0, 0, alpha)
        score_stage(tile_at(jnp.minimum(t + 2, nfar - 1)), 0)
        ms, alpha = _softmax_stage(s_ref, p_ref, 1, nch, far_bias(i1), ms)
        value_stage(i1, 1, alpha)
        return ms

    neg = jnp.full((1, CHAIN_W), NEG_BIG, F32)
    ms = lax.fori_loop(0, nfar // 2, far_body, (neg,) * nch)

    def near_body(t, ms):
        i = tile_at(t)
        off = pl.multiple_of(i * tk, tk)

        def fill(n, _):
            tt, cc = n // nc, n % nc
            u = i * nsub + tt
            cq = qi * nc + cc
            blk_after = smin_ref[u] - cmax_ref[cq] >= REL_MAX_DISTANCE
            blk_before = smax_ref[u] - cmin_ref[cq] <= -REL_MAX_DISTANCE
            roff = pl.multiple_of(tt * sk, sk)
            coff = pl.multiple_of(cc * CHAIN_W, CHAIN_W)

            @pl.when(jnp.logical_or(blk_after, blk_before))
            def _():
                c = jnp.where(blk_after, b_after, b_before)
                bias_ref[pl.ds(roff, sk), pl.ds(coff, CHAIN_W)] = jnp.full((sk, CHAIN_W), c, F32)

            @pl.when(jnp.logical_not(jnp.logical_or(blk_after, blk_before)))
            def _():
                pk = posk_ref[pl.ds(pl.multiple_of(off + roff, sk), sk), :]
                pk = jnp.concatenate([pk] * (CHAIN_W // LANE), axis=1)
                rel = pk - posq_ref[:, pl.ds(coff, CHAIN_W)]
                bias_ref[pl.ds(roff, sk), pl.ds(coff, CHAIN_W)] = _t5_bias(rel, tab_ref, h)

            return 0

        lax.fori_loop(0, nsub * nc, fill, 0)
        score_stage(i, 0)
        ms, alpha = _softmax_stage(s_ref, p_ref, 0, nch, None, ms,
                                   bias=lambda j: bias_ref[:, (j // 2) * CHAIN_W:(j // 2 + 1) * CHAIN_W])
        value_stage(i, 0, alpha)
        return ms

    lax.fori_loop(nfar, nk, near_body, ms)

    lam = lam_ref[0]
    dv = DIFF_V_DIM
    for c in range(nc):
        o1 = acc_ref[2 * c, :dv] / acc_ref[2 * c, dv:dv + 1]
        o2 = acc_ref[2 * c + 1, :dv] / acc_ref[2 * c + 1, dv:dv + 1]
        o = o1 - lam * o2
        msq = jnp.mean(o * o, axis=0, keepdims=True)
        o = o * lax.rsqrt(msq + DIFF_SUBLN_EPS) * g_ref[...]
        o_ref[:, c * CHAIN_W:(c + 1) * CHAIN_W] = (o * out_scale).astype(o_ref.dtype)


def diff_attention(qT, qkv, vT, positions, rel_bias, lam, subln, lam_init, nc=4, tk=512, sk=128):
    s = qkv.shape[0]
    tq = _pick(s, nc * CHAIN_W, CHAIN_W)
    nc = tq // CHAIN_W
    tk = _pick(s, tk)
    sk = _pick(tk, sk)
    nk, nq = s // tk, s // tq
    pos = positions.reshape(s).astype(jnp.int32)
    kmin = pos.reshape(nk, tk).min(axis=1)
    kmax = pos.reshape(nk, tk).max(axis=1)
    smin = pos.reshape(s // sk, sk).min(axis=1)
    smax = pos.reshape(s // sk, sk).max(axis=1)
    qmin = pos.reshape(nq, tq).min(axis=1)
    qmax = pos.reshape(nq, tq).max(axis=1)
    cmin = pos.reshape(s // CHAIN_W, CHAIN_W).min(axis=1)
    cmax = pos.reshape(s // CHAIN_W, CHAIN_W).max(axis=1)
    far = jnp.logical_or(kmin[None, :] - qmax[:, None] >= REL_MAX_DISTANCE,
                         kmax[None, :] - qmin[:, None] <= -REL_MAX_DISTANCE)
    order = jnp.argsort(jnp.logical_not(far), axis=1, stable=True).astype(jnp.int32)
    nfar = far.sum(axis=1).astype(jnp.int32)
    nfar = nfar - nfar % 2
    posq = pos.reshape(1, s)
    posk = jnp.broadcast_to(pos.reshape(s, 1), (s, LANE))
    tab = (rel_bias.T * LOG2E).astype(F32)
    dva = DIFF_V_DIM + SUM_ROWS
    smem = pl.BlockSpec(memory_space=pltpu.SMEM)
    grid_spec = pltpu.PrefetchScalarGridSpec(
        num_scalar_prefetch=10,
        grid=(DIFF_HEADS, nq),
        in_specs=[smem, smem,
                  pl.BlockSpec((DIFF_V_DIM, tq), lambda h, i, *_: (h, i)),
                  pl.BlockSpec((s, 2 * DIFF_HEAD_DIM), lambda h, i, *_: (0, DIFF_HEADS + h)),
                  pl.BlockSpec((DIFF_V_DIM, s), lambda h, i, *_: (h, 0)),
                  pl.BlockSpec((1, tq), lambda h, i, *_: (0, i)),
                  pl.BlockSpec((s, LANE), lambda h, i, *_: (0, 0)),
                  pl.BlockSpec((DIFF_V_DIM, 1), lambda h, i, *_: (0, 0))],
        out_specs=pl.BlockSpec((DIFF_V_DIM, tq), lambda h, i, *_: (h, i)),
        scratch_shapes=[pltpu.VMEM((2, DIFF_V_DIM, tq), BF16),
                        pltpu.VMEM((2 * nc, dva, CHAIN_W), F32),
                        pltpu.VMEM((2, 2 * nc, tk, CHAIN_W), F32),
                        pltpu.VMEM((2, 2 * nc, tk, CHAIN_W), BF16),
                        pltpu.VMEM((tk, tq), F32)],
    )
    return pl.pallas_call(
        functools.partial(_diff_attn_body, tk=tk, nk=nk, nc=nc, sk=sk, out_scale=1.0 - lam_init),
        grid_spec=grid_spec,
        out_shape=jax.ShapeDtypeStruct((DIFF_HEADS * DIFF_V_DIM, s), BF16),
        compiler_params=_cparams(("parallel", "parallel")),
        name="diff_attention",
    )(order.reshape(nq * nk), nfar, kmin, kmax, smin, smax, qmin, qmax, cmin, cmax,
      tab, lam.reshape(1).astype(F32),
      qT, qkv, vT, posq, posk,
      subln.reshape(DIFF_V_DIM, 1).astype(F32))


def _xa_body(x_ref, gpre_ref, wq_ref, kT_ref, v_ref, wo_ref, gpost_ref, gnext_ref, o_ref, h_ref):
    x = x_ref[...]
    h = (_rms(x, NORM_EPS) * gpre_ref[...]).astype(BF16)
    q = jnp.dot(h, wq_ref[...], preferred_element_type=F32) * (MEM_HEAD_DIM ** -0.5)
    q = q.astype(BF16)
    outs = []
    for hd in range(MEM_HEADS):
        sl = slice(hd * MEM_HEAD_DIM, (hd + 1) * MEM_HEAD_DIM)
        s = jnp.dot(q[:, sl], kT_ref[sl, :], preferred_element_type=F32)
        p = jnp.exp(s - jnp.max(s, axis=-1, keepdims=True))
        l = jnp.sum(p, axis=-1, keepdims=True)
        o = jnp.dot(p.astype(BF16), v_ref[:, sl], preferred_element_type=F32) / l
        outs.append(o.astype(BF16))
    o = jnp.concatenate(outs, axis=1)
    y = jnp.dot(o, wo_ref[...], preferred_element_type=F32)
    x = x + _rms(y, NORM_EPS) * gpost_ref[...]
    o_ref[...] = x
    h_ref[...] = (_rms(x, NORM_EPS) * gnext_ref[...]).astype(h_ref.dtype)


def cross_attention_sublayer(x, g_pre, w_q, kT, v, w_o, l, g_post, g_next):
    s, d = x.shape
    ts = _pick(s, 256, 8)
    row = lambda i: (i, 0)
    fixed = lambda i: (0, 0)
    return pl.pallas_call(
        _xa_body,
        grid=(s // ts,),
        in_specs=[pl.BlockSpec((ts, d), row), pl.BlockSpec((1, d), fixed),
                  pl.BlockSpec((None,) + w_q.shape[1:], lambda i: (l, 0, 0)), pl.BlockSpec(kT.shape, fixed),
                  pl.BlockSpec(v.shape, fixed), pl.BlockSpec((None,) + w_o.shape[1:], lambda i: (l, 0, 0)),
                  pl.BlockSpec((1, d), fixed), pl.BlockSpec((1, d), fixed)],
        out_specs=[pl.BlockSpec((ts, d), row), pl.BlockSpec((ts, d), row)],
        out_shape=[jax.ShapeDtypeStruct((s, d), F32), jax.ShapeDtypeStruct((s, d), BF16)],
        compiler_params=_cparams(("parallel",)),
        name="cross_attention",
    )(x, g_pre.reshape(1, d), w_q, kT, v, w_o, g_post.reshape(1, d), g_next.reshape(1, d))


def _rot_cols(w):
    half = w.shape[-1] // 2
    return jnp.concatenate([-w[..., half:], w[..., :half]], axis=-1)


def _prep_w_in(w, q_rank, kv_rank):
    a = q_rank + kv_rank
    kr = w[..., a:a + MLA_ROPE_DIM]
    z = jnp.zeros_like(kr)
    w_lat = jnp.concatenate([w[..., :a], kr, z, _rot_cols(kr), z], axis=-1)
    b = a + MLA_ROPE_DIM
    nq = DIFF_HEADS * DIFF_V_DIM
    w_q = w[..., b:b + nq] * (LOG2E * DIFF_HEAD_DIM ** -0.5)
    w_qkvd = jnp.concatenate([w_q, w[..., b + nq:b + 3 * nq]], axis=-1)
    return w_lat.astype(BF16), w_qkvd.astype(BF16), w[..., b + 3 * nq:].astype(BF16)


def _prep_w_uq(w):
    nl, r, _ = w.shape
    w4 = w.reshape(nl, r, MLA_HEADS, MLA_QK_DIM)
    nope, rope = w4[..., :MLA_NOPE_DIM], w4[..., MLA_NOPE_DIM:]
    z = jnp.zeros_like(rope)
    out = jnp.concatenate([nope, rope, z, _rot_cols(rope), z], axis=-1)
    return out.reshape(nl, r, -1).astype(BF16)


def _prep_w_ukv(w):
    nl, r, _ = w.shape
    w4 = w.reshape(nl, r, MLA_HEADS, MLA_NOPE_DIM + MLA_V_DIM)
    k_nope = w4[..., :MLA_NOPE_DIM].reshape(nl, r, -1)
    v = w4[..., MLA_NOPE_DIM:].reshape(nl, r, -1)
    return jnp.concatenate([k_nope, v], axis=-1).astype(BF16)


def _rope_tables(positions, s):
    half = MLA_ROPE_DIM // 2
    inv_freq = ROPE_BASE ** (-jnp.arange(half, dtype=F32) / half)
    ang = positions.reshape(s, 1).astype(F32) * inv_freq
    z = jnp.zeros((s, LANE - MLA_ROPE_DIM), F32)
    cos = jnp.concatenate([jnp.cos(ang), jnp.cos(ang), z], axis=1)
    sin = jnp.concatenate([jnp.sin(ang), jnp.sin(ang), z], axis=1)
    return cos, sin


def kernel(x, mem, positions, rel_bias, mix_norm_pre, mix_norm_post, w_in, mla_q_norm, mla_w_uq, mla_kv_norm, mla_w_ukv, diff_lambda, diff_subln, w_mla_branch, w_diff_branch, w_out, xa_norm_pre, xa_norm_post, xa_mem_norm, xa_w_q, xa_w_kv, xa_w_o, ffn_norm_pre, ffn_norm_post, ffn_w_in, ffn_w_out):
    b, s, d = x.shape
    assert b == 1, "kernel is written for batch 1"
    depth = w_in.shape[0]
    q_rank, kv_rank = mla_q_norm.shape[1], mla_kv_norm.shape[1]
    dff = ffn_w_out.shape[1]
    dff_pad = -(-dff // 512) * 512
    nh = DIFF_HEADS * DIFF_V_DIM

    xs = x.reshape(s, d)
    mems = mem.reshape(mem.shape[1], d)
    cos, sin = _rope_tables(positions, s)

    w_lat, w_qkvd, w_gate = _prep_w_in(w_in, q_rank, kv_rank)
    w_uq, w_ukv = _prep_w_uq(mla_w_uq), _prep_w_ukv(mla_w_ukv)
    w_mb, w_db, w_o = w_mla_branch.astype(BF16), w_diff_branch.astype(BF16), w_out.astype(BF16)
    w_xq, w_xkv, w_xo = xa_w_q.astype(BF16), xa_w_kv.astype(BF16), xa_w_o.astype(BF16)
    pad = dff_pad - dff
    w_gu = jnp.concatenate([jnp.pad(ffn_w_in[..., :dff], ((0, 0), (0, 0), (0, pad))),
                            jnp.pad(ffn_w_in[..., dff:], ((0, 0), (0, 0), (0, pad)))], axis=-1).astype(BF16)
    w_fo = jnp.pad(ffn_w_out, ((0, 0), (0, pad), (0, 0))).astype(BF16)

    h = rmsnorm(xs, mix_norm_pre[0], BF16)
    for l in range(depth):
        lat = matmul(h, w_lat, l, F32, bn=768)
        qkvd = matmul(h, w_qkvd, l, BF16)
        gates = matmul(h, w_gate, l, BF16)

        q, k, v = mla_prep(lat, mla_q_norm[l], mla_kv_norm[l], w_uq, w_ukv, l, cos, sin)
        o_mla = mla_attention(q.T, k, v.T).T

        lam_init = 0.8 - 0.6 * math.exp(-0.3 * l)
        lv = diff_lambda[l].astype(F32)
        lam = jnp.exp(jnp.sum(lv[0] * lv[1])) - jnp.exp(jnp.sum(lv[2] * lv[3])) + lam_init
        qT_d = qkvd[:, :nh].T
        vT_d = qkvd[:, 2 * nh:].T
        o_diff = diff_attention(qT_d, qkvd, vT_d, positions, rel_bias, lam, diff_subln[l], lam_init).T

        merged = gated_merge(o_mla, o_diff, w_mb, w_db, l, gates)
        y = matmul(merged, w_o, l, F32)
        xs = add_rmsnorm(xs, y, mix_norm_post[l])

        mem_n = rmsnorm(mems, xa_mem_norm[l], BF16)
        kv = matmul(mem_n, w_xkv, l, BF16)
        hm = MEM_HEADS * MEM_HEAD_DIM
        xs, h = cross_attention_sublayer(xs, xa_norm_pre[l], w_xq, kv[:, :hm].T, kv[:, hm:], w_xo, l,
                                         xa_norm_post[l], ffn_norm_pre[l])

        act = swiglu_matmul(h, w_gu, l, dff_pad)
        y = matmul(act, w_fo, l, F32, bk=2816)
        if l + 1 < depth:
            xs, h = add_rmsnorm(xs, y, ffn_norm_post[l], mix_norm_pre[l + 1])
        else:
            xs = add_rmsnorm(xs, y, ffn_norm_post[l])

    return xs.reshape(b, s, d)
```

```python
import functools
import math

import jax
import jax.numpy as jnp
from jax import lax
from jax.experimental import pallas as pl
from jax.experimental.pallas import tpu as pltpu

F32 = jnp.float32
BF16 = jnp.bfloat16

MLA_HEADS = 8
MLA_NOPE_DIM = 128
MLA_ROPE_DIM = 64
MLA_V_DIM = 128
MLA_QK_DIM = MLA_NOPE_DIM + MLA_ROPE_DIM
DIFF_HEADS = 8
DIFF_HEAD_DIM = 64
DIFF_V_DIM = 2 * DIFF_HEAD_DIM
MEM_HEADS = 4
MEM_HEAD_DIM = 128
REL_BUCKETS = 32
REL_MAX_DISTANCE = 128
ROPE_BASE = 10000.0
NORM_EPS = 1e-6
DIFF_SUBLN_EPS = 1e-5

LANE = 128
VMEM_LIMIT_BYTES = 56 * 1024 * 1024
NEG_BIG = -1e30
LOG2E = math.log2(math.e)


def _cparams(sem):
    return pltpu.CompilerParams(dimension_semantics=sem, vmem_limit_bytes=VMEM_LIMIT_BYTES)


def _pick(dim, target, align=LANE):
    if dim <= target:
        return dim
    t = (target // align) * align
    while t >= align:
        if dim % t == 0:
            return t
        t -= align
    return dim


def _sigmoid(x):
    return 1.0 / (1.0 + jnp.exp(-x))


def _rms(x, eps):
    return x * lax.rsqrt(jnp.mean(x * x, axis=-1, keepdims=True) + eps)


def _rmsnorm_body(x_ref, g_ref, o_ref, *, eps):
    x = x_ref[...].astype(F32)
    o_ref[...] = (_rms(x, eps) * g_ref[...]).astype(o_ref.dtype)


def rmsnorm(x, g, out_dtype, eps=NORM_EPS):
    m, d = x.shape
    bm = _pick(m, 256, 8)
    return pl.pallas_call(
        functools.partial(_rmsnorm_body, eps=eps),
        grid=(m // bm,),
        in_specs=[pl.BlockSpec((bm, d), lambda i: (i, 0)), pl.BlockSpec((1, d), lambda i: (0, 0))],
        out_specs=pl.BlockSpec((bm, d), lambda i: (i, 0)),
        out_shape=jax.ShapeDtypeStruct((m, d), out_dtype),
        compiler_params=_cparams(("parallel",)),
        name="rmsnorm",
    )(x, g.reshape(1, d))


def _add_rmsnorm_body(x_ref, y_ref, g_ref, *rest, eps, with_next):
    y = y_ref[...].astype(F32)
    x = x_ref[...] + _rms(y, eps) * g_ref[...]
    if with_next:
        gn_ref, o_ref, h_ref = rest
        h_ref[...] = (_rms(x, eps) * gn_ref[...]).astype(h_ref.dtype)
    else:
        (o_ref,) = rest
    o_ref[...] = x


def add_rmsnorm(x, y, g, g_next=None, eps=NORM_EPS):
    m, d = x.shape
    bm = _pick(m, 256, 8)
    row = pl.BlockSpec((bm, d), lambda i: (i, 0))
    vec = pl.BlockSpec((1, d), lambda i: (0, 0))
    with_next = g_next is not None
    args = (x, y, g.reshape(1, d)) + ((g_next.reshape(1, d),) if with_next else ())
    return pl.pallas_call(
        functools.partial(_add_rmsnorm_body, eps=eps, with_next=with_next),
        grid=(m // bm,),
        in_specs=[row, row, vec] + ([vec] if with_next else []),
        out_specs=[row, row] if with_next else row,
        out_shape=([jax.ShapeDtypeStruct((m, d), F32), jax.ShapeDtypeStruct((m, d), BF16)] if with_next
                   else jax.ShapeDtypeStruct((m, d), F32)),
        compiler_params=_cparams(("parallel",)),
        name="add_rmsnorm",
    )(*args)


def _matmul_body(a_ref, w_ref, o_ref, *scratch, nk):
    part = jnp.dot(a_ref[...], w_ref[...], preferred_element_type=F32)
    if nk == 1:
        o_ref[...] = part.astype(o_ref.dtype)
        return
    (acc_ref,) = scratch
    k = pl.program_id(2)

    @pl.when(k == 0)
    def _():
        acc_ref[...] = part

    @pl.when(k > 0)
    def _():
        acc_ref[...] += part

    @pl.when(k == nk - 1)
    def _():
        o_ref[...] = acc_ref[...].astype(o_ref.dtype)


def matmul(a, w, l, out_dtype, bm=1024, bn=1024, bk=4096):
    m, kd = a.shape
    _, _, n = w.shape
    bm, bn, bk = _pick(m, bm, 8), _pick(n, bn), _pick(kd, bk)
    nk = kd // bk
    return pl.pallas_call(
        functools.partial(_matmul_body, nk=nk),
        grid=(m // bm, n // bn, nk),
        in_specs=[pl.BlockSpec((bm, bk), lambda i, j, k: (i, k)),
                  pl.BlockSpec((None, bk, bn), lambda i, j, k: (l, k, j))],
        out_specs=pl.BlockSpec((bm, bn), lambda i, j, k: (i, j)),
        out_shape=jax.ShapeDtypeStruct((m, n), out_dtype),
        scratch_shapes=[pltpu.VMEM((bm, bn), F32)] if nk > 1 else [],
        compiler_params=_cparams(("parallel", "parallel", "arbitrary")),
        name="matmul",
    )(a, w)


def _swiglu_body(a_ref, wg_ref, wu_ref, o_ref):
    a = a_ref[...]
    g = jnp.dot(a, wg_ref[...], preferred_element_type=F32)
    u = jnp.dot(a, wu_ref[...], preferred_element_type=F32)
    o_ref[...] = (g * _sigmoid(g) * u).astype(o_ref.dtype)


def swiglu_matmul(a, w_gu, l, dff, bm=1024, bn=512):
    m, kd = a.shape
    bm, bn = _pick(m, bm, 8), _pick(dff, bn)
    nj = dff // bn
    return pl.pallas_call(
        _swiglu_body,
        grid=(m // bm, nj),
        in_specs=[pl.BlockSpec((bm, kd), lambda i, j: (i, 0)),
                  pl.BlockSpec((None, kd, bn), lambda i, j: (l, 0, j)),
                  pl.BlockSpec((None, kd, bn), lambda i, j: (l, 0, j + nj))],
        out_specs=pl.BlockSpec((bm, bn), lambda i, j: (i, j)),
        out_shape=jax.ShapeDtypeStruct((m, dff), BF16),
        compiler_params=_cparams(("parallel", "parallel")),
        name="swiglu_matmul",
    )(a, w_gu, w_gu)


def _merge_body(om_ref, od_ref, wm_ref, wd_ref, gm_ref, gd_ref, o_ref):
    ym = jnp.dot(om_ref[...], wm_ref[...], preferred_element_type=F32)
    yd = jnp.dot(od_ref[...], wd_ref[...], preferred_element_type=F32)
    sm = _sigmoid(gm_ref[...].astype(F32))
    sd = _sigmoid(gd_ref[...].astype(F32))
    o_ref[...] = (sm * ym + sd * yd).astype(o_ref.dtype)


def gated_merge(o_mla, o_diff, w_mla, w_diff, l, gates, bm=1024, bn=1024):
    m, km = o_mla.shape
    _, kd = o_diff.shape
    _, _, n = w_mla.shape
    bm, bn = _pick(m, bm, 8), _pick(n, bn)
    nj = n // bn
    return pl.pallas_call(
        _merge_body,
        grid=(m // bm, nj),
        in_specs=[pl.BlockSpec((bm, km), lambda i, j: (i, 0)),
                  pl.BlockSpec((bm, kd), lambda i, j: (i, 0)),
                  pl.BlockSpec((None, km, bn), lambda i, j: (l, 0, j)),
                  pl.BlockSpec((None, kd, bn), lambda i, j: (l, 0, j)),
                  pl.BlockSpec((bm, bn), lambda i, j: (i, j)),
                  pl.BlockSpec((bm, bn), lambda i, j: (i, j + nj))],
        out_specs=pl.BlockSpec((bm, bn), lambda i, j: (i, j)),
        out_shape=jax.ShapeDtypeStruct((m, n), BF16),
        compiler_params=_cparams(("parallel", "parallel")),
        name="gated_merge",
    )(o_mla, o_diff, w_mla, w_diff, gates, gates)


def _mla_prep_body(lat_ref, gq_ref, gkv_ref, wq_ref, wkv_ref, cos_ref, sin_ref,
                   q_ref, k_ref, v_ref, *, q_rank, kv_rank, scale):
    cos = cos_ref[...]
    sin = sin_ref[...]
    cq = lat_ref[:, :q_rank].astype(F32)
    ckv = lat_ref[:, q_rank:q_rank + kv_rank].astype(F32)
    kr = lat_ref[:, q_rank + kv_rank:q_rank + kv_rank + LANE].astype(F32)
    kr_rot = lat_ref[:, q_rank + kv_rank + LANE:q_rank + kv_rank + 2 * LANE].astype(F32)
    k_rope = (kr * cos + kr_rot * sin).astype(BF16)

    cqn = (_rms(cq, NORM_EPS) * gq_ref[...]).astype(BF16)
    qa = jnp.dot(cqn, wq_ref[...], preferred_element_type=F32)
    ckvn = (_rms(ckv, NORM_EPS) * gkv_ref[...]).astype(BF16)
    kv = jnp.dot(ckvn, wkv_ref[...], preferred_element_type=F32)

    hw = MLA_NOPE_DIM + 2 * LANE
    for h in range(MLA_HEADS):
        base = h * hw
        nope = qa[:, base:base + MLA_NOPE_DIM]
        rp = qa[:, base + MLA_NOPE_DIM:base + MLA_NOPE_DIM + LANE]
        rr = qa[:, base + MLA_NOPE_DIM + LANE:base + hw]
        q_ref[:, h * 256:h * 256 + 128] = (nope * scale).astype(BF16)
        q_ref[:, h * 256 + 128:(h + 1) * 256] = ((rp * cos + rr * sin) * scale).astype(BF16)
        k_ref[:, h * 256:h * 256 + 128] = kv[:, h * 128:(h + 1) * 128].astype(BF16)
        k_ref[:, h * 256 + 128:(h + 1) * 256] = k_rope
    v_ref[...] = kv[:, MLA_HEADS * MLA_NOPE_DIM:].astype(BF16)


def mla_prep(lat, gq, gkv, wq, wkv, l, cos, sin):
    s, lw = lat.shape
    q_rank, kv_rank = gq.shape[0], gkv.shape[0]
    ts = _pick(s, 256, 8)
    hq = MLA_HEADS * 256
    hv = MLA_HEADS * MLA_V_DIM
    row = lambda i: (i, 0)
    fixed = lambda i: (0, 0)
    return pl.pallas_call(
        functools.partial(_mla_prep_body, q_rank=q_rank, kv_rank=kv_rank, scale=LOG2E * MLA_QK_DIM ** -0.5),
        grid=(s // ts,),
        in_specs=[pl.BlockSpec((ts, lw), row),
                  pl.BlockSpec((1, q_rank), fixed), pl.BlockSpec((1, kv_rank), fixed),
                  pl.BlockSpec((None,) + wq.shape[1:], lambda i: (l, 0, 0)),
                  pl.BlockSpec((None,) + wkv.shape[1:], lambda i: (l, 0, 0)),
                  pl.BlockSpec((ts, LANE), row), pl.BlockSpec((ts, LANE), row)],
        out_specs=[pl.BlockSpec((ts, hq), row), pl.BlockSpec((ts, hq), row), pl.BlockSpec((ts, hv), row)],
        out_shape=[jax.ShapeDtypeStruct((s, hq), BF16), jax.ShapeDtypeStruct((s, hq), BF16),
                   jax.ShapeDtypeStruct((s, hv), BF16)],
        compiler_params=_cparams(("parallel",)),
        name="mla_prep",
    )(lat, gq.reshape(1, -1), gkv.reshape(1, -1), wq, wkv, cos, sin)


CHAIN_W = 256
SUM_ROWS = 16


def _softmax_stage(s_ref, p_ref, slot, n, cs, ms, bias=None):
    m_new, alpha = [], []
    for j in range(n):
        s = s_ref[slot, j]
        if bias is not None:
            s = s + bias(j)
        t = jnp.max(s, axis=0, keepdims=True)
        if cs is None:
            shift = m = jnp.maximum(ms[j], t)
        else:
            shift = jnp.maximum(ms[j] - cs, t)
            m = shift + cs
        p_ref[slot, j] = jnp.exp2(s - shift).astype(BF16)
        alpha.append(jnp.exp2(ms[j] - m))
        m_new.append(m)
    return tuple(m_new), tuple(alpha)


def _value_stage(acc_ref, p_ref, slot, n, alpha, vt):
    vt = jnp.concatenate([vt, jnp.ones((SUM_ROWS, vt.shape[1]), vt.dtype)], axis=0)
    for j in range(n):
        acc_ref[j] = alpha[j] * acc_ref[j] + jnp.dot(vt, p_ref[slot, j], preferred_element_type=F32)


def _mla_attn_body(qT_ref, k_ref, vT_ref, o_ref, acc_ref, s_ref, p_ref, *, tk, nk, nc, group_n):
    acc_ref[...] = jnp.zeros_like(acc_ref)

    def score_stage(i, slot):
        kt = k_ref[pl.ds(pl.multiple_of(i * tk, tk), tk), :]
        for c in range(nc):
            s_ref[slot, c] = jnp.dot(kt, qT_ref[:, c * CHAIN_W:(c + 1) * CHAIN_W], preferred_element_type=F32)

    def value_stage(i, slot, alpha):
        vt = vT_ref[:, pl.ds(pl.multiple_of(i * tk, tk), tk)]
        _value_stage(acc_ref, p_ref, slot, nc, alpha, vt)

    score_stage(0, 0)

    def group(g, ms):
        for u in range(group_n):
            i = g * group_n + u
            slot = u % 2
            score_stage(jnp.minimum(i + 1, nk - 1), 1 - slot)
            ms, alpha = _softmax_stage(s_ref, p_ref, slot, nc, None, ms)
            value_stage(i, slot, alpha)
        return ms

    neg = jnp.full((1, CHAIN_W), NEG_BIG, F32)
    lax.fori_loop(0, nk // group_n, group, (neg,) * nc)
    for c in range(nc):
        o = acc_ref[c, :MLA_V_DIM] / acc_ref[c, MLA_V_DIM:MLA_V_DIM + 1]
        o_ref[:, c * CHAIN_W:(c + 1) * CHAIN_W] = o.astype(o_ref.dtype)


def mla_attention(qT, k, vT, nc=8, tk=1024):
    s = k.shape[0]
    tq = _pick(s, nc * CHAIN_W, CHAIN_W)
    nc = tq // CHAIN_W
    tk = _pick(s // 2, tk)
    nk = s // tk
    group_n = 2
    assert nk % group_n == 0
    dva = MLA_V_DIM + SUM_ROWS
    return pl.pallas_call(
        functools.partial(_mla_attn_body, tk=tk, nk=nk, nc=nc, group_n=group_n),
        grid=(MLA_HEADS, s // tq),
        in_specs=[pl.BlockSpec((256, tq), lambda h, i: (h, i)),
                  pl.BlockSpec((s, 256), lambda h, i: (0, h)),
                  pl.BlockSpec((MLA_V_DIM, s), lambda h, i: (h, 0))],
        out_specs=pl.BlockSpec((MLA_V_DIM, tq), lambda h, i: (h, i)),
        out_shape=jax.ShapeDtypeStruct((MLA_HEADS * MLA_V_DIM, s), BF16),
        scratch_shapes=[pltpu.VMEM((nc, dva, CHAIN_W), F32),
                        pltpu.VMEM((2, nc, tk, CHAIN_W), F32),
                        pltpu.VMEM((2, nc, tk, CHAIN_W), BF16)],
        compiler_params=_cparams(("parallel", "parallel")),
        name="mla_attention",
    )(qT, k, vT)


T5_BUCKET_EDGES = (0, 1, 2, 3, 4, 5, 6, 7, 8, 12, 16, 23, 32, 46, 64, 91)


def _t5_bias(rel, tab_ref, h):
    nb = REL_BUCKETS // 2
    pos = rel > 0
    n = jnp.abs(rel)
    v = jnp.where(pos, tab_ref[h, nb], tab_ref[h, 0])
    for b in range(1, nb):
        leaf = jnp.where(pos, tab_ref[h, nb + b], tab_ref[h, b])
        v = jnp.where(n >= T5_BUCKET_EDGES[b], leaf, v)
    return v


def _diff_attn_body(order_ref, nfar_ref, kmin_ref, kmax_ref, smin_ref, smax_ref, qmin_ref, qmax_ref,
                    cmin_ref, cmax_ref,
                    tab_ref, lam_ref, qT_ref, k_ref, vT_ref, posq_ref, posk_ref, g_ref,
                    o_ref, qm_ref, acc_ref, s_ref, p_ref, bias_ref, *, tk, nk, nc, sk, out_scale):
    h = pl.program_id(0)
    qi = pl.program_id(1)
    nch = 2 * nc
    qT = qT_ref[...]
    row = lax.broadcasted_iota(jnp.int32, qT.shape, 0)
    zero = jnp.zeros_like(qT)
    qm_ref[0] = jnp.where(row < DIFF_HEAD_DIM, qT, zero)
    qm_ref[1] = jnp.where(row >= DIFF_HEAD_DIM, qT, zero)
    b_after = tab_ref[h, REL_BUCKETS - 1]
    b_before = tab_ref[h, REL_BUCKETS // 2 - 1]
    q_lo = qmin_ref[qi]
    q_hi = qmax_ref[qi]
    acc_ref[...] = jnp.zeros_like(acc_ref)
    nsub = tk // sk
    nfar = nfar_ref[qi]

    def tile_at(t):
        return order_ref[qi * nk + t]

    def score_stage(i, slot):
        kt = k_ref[pl.ds(pl.multiple_of(i * tk, tk), tk), :]
        for j in range(nch):
            q = qm_ref[j % 2, :, (j // 2) * CHAIN_W:(j // 2 + 1) * CHAIN_W]
            s_ref[slot, j] = jnp.dot(kt, q, preferred_element_type=F32)

    def value_stage(i, slot, alpha):
        vt = vT_ref[:, pl.ds(pl.multiple_of(i * tk, tk), tk)]
        _value_stage(acc_ref, p_ref, slot, nch, alpha, vt)

    def far_bias(i):
        return jnp.where(kmin_ref[i] - q_hi >= REL_MAX_DISTANCE, b_after, b_before)

    @pl.when(nfar > 0)
    def _():
        score_stage(tile_at(0), 0)

    def far_tile_at(t):
        return tile_at(jnp.minimum(t, nfar - 1))

    def far_group(t0, n, ms):
        tiles = [far_tile_at(t0 + u) for u in range(n + 1)]
        for u in range(n):
            slot = u % 2
            score_stage(tiles[u + 1], 1 - slot)
            ms, alpha = _softmax_stage(s_ref, p_ref, slot, nch, far_bias(tiles[u]), ms)
            value_stage(tiles[u], slot, alpha)
        return ms

    neg = jnp.full((1, CHAIN_W), NEG_BIG, F32)
    ms = lax.fori_loop(0, nfar // 8, lambda g, ms: far_group(8 * g, 8, ms), (neg,) * nch)
    ms = lax.fori_loop(0, (nfar % 8) // 4, lambda g, ms: far_group(nfar // 8 * 8, 4, ms), ms)
    ms = lax.fori_loop(0, (nfar % 4) // 2, lambda g, ms: far_group(nfar - 2, 2, ms), ms)

    def fill_bias(i, slot):
        off = pl.multiple_of(i * tk, tk)

        def fill(n, _):
            tt, cc = n // nc, n % nc
            u = i * nsub + tt
            cq = qi * nc + cc
            blk_after = smin_ref[u] - cmax_ref[cq] >= REL_MAX_DISTANCE
            blk_before = smax_ref[u] - cmin_ref[cq] <= -REL_MAX_DISTANCE
            roff = pl.multiple_of(tt * sk, sk)
            coff = pl.multiple_of(cc * CHAIN_W, CHAIN_W)

            @pl.when(jnp.logical_or(blk_after, blk_before))
            def _():
                c = jnp.where(blk_after, b_after, b_before)
                bias_ref[slot, pl.ds(roff, sk), pl.ds(coff, CHAIN_W)] = jnp.full((sk, CHAIN_W), c, F32)

            @pl.when(jnp.logical_not(jnp.logical_or(blk_after, blk_before)))
            def _():
                pk = posk_ref[pl.ds(pl.multiple_of(off + roff, sk), sk), :]
                pk = jnp.concatenate([pk] * (CHAIN_W // LANE), axis=1)
                rel = pk - posq_ref[:, pl.ds(coff, CHAIN_W)]
                bias_ref[slot, pl.ds(roff, sk), pl.ds(coff, CHAIN_W)] = _t5_bias(rel, tab_ref, h)

            return 0

        lax.fori_loop(0, nsub * nc, fill, 0)

    def near_stage(i, slot, ms):
        ms, alpha = _softmax_stage(s_ref, p_ref, slot, nch, None, ms,
                                   bias=lambda j: bias_ref[slot, :, (j // 2) * CHAIN_W:(j // 2 + 1) * CHAIN_W])
        value_stage(i, slot, alpha)
        return ms

    def near_pair(g, ms):
        ia, ib = tile_at(nfar + 2 * g), tile_at(nfar + 2 * g + 1)
        fill_bias(ia, 0)
        fill_bias(ib, 1)
        score_stage(ia, 0)
        score_stage(ib, 1)
        ms = near_stage(ia, 0, ms)
        return near_stage(ib, 1, ms)

    def near_single(g, ms):
        i = tile_at(nk - 1)
        fill_bias(i, 0)
        score_stage(i, 0)
        return near_stage(i, 0, ms)

    ms = lax.fori_loop(0, (nk - nfar) // 2, near_pair, ms)
    lax.fori_loop(0, (nk - nfar) % 2, near_single, ms)

    lam = lam_ref[0]
    dv = DIFF_V_DIM
    for c in range(nc):
        o1 = acc_ref[2 * c, :dv] / acc_ref[2 * c, dv:dv + 1]
        o2 = acc_ref[2 * c + 1, :dv] / acc_ref[2 * c + 1, dv:dv + 1]
        o = o1 - lam * o2
        msq = jnp.mean(o * o, axis=0, keepdims=True)
        o = o * lax.rsqrt(msq + DIFF_SUBLN_EPS) * g_ref[...]
        o_ref[:, c * CHAIN_W:(c + 1) * CHAIN_W] = (o * out_scale).astype(o_ref.dtype)


def diff_attention(qT, qkv, vT, positions, rel_bias, lam, subln, lam_init, nc=4, tk=512, sk=128):
    s = qkv.shape[0]
    tq = _pick(s, nc * CHAIN_W, CHAIN_W)
    nc = tq // CHAIN_W
    tk = _pick(s, tk)
    sk = _pick(tk, sk)
    nk, nq = s // tk, s // tq
    pos = positions.reshape(s).astype(jnp.int32)
    kmin = pos.reshape(nk, tk).min(axis=1)
    kmax = pos.reshape(nk, tk).max(axis=1)
    smin = pos.reshape(s // sk, sk).min(axis=1)
    smax = pos.reshape(s // sk, sk).max(axis=1)
    qmin = pos.reshape(nq, tq).min(axis=1)
    qmax = pos.reshape(nq, tq).max(axis=1)
    cmin = pos.reshape(s // CHAIN_W, CHAIN_W).min(axis=1)
    cmax = pos.reshape(s // CHAIN_W, CHAIN_W).max(axis=1)
    far = jnp.logical_or(kmin[None, :] - qmax[:, None] >= REL_MAX_DISTANCE,
                         kmax[None, :] - qmin[:, None] <= -REL_MAX_DISTANCE)
    order = jnp.argsort(jnp.logical_not(far), axis=1, stable=True).astype(jnp.int32)
    nfar = far.sum(axis=1).astype(jnp.int32)
    nfar = nfar - nfar % 2
    posq = pos.reshape(1, s)
    posk = jnp.broadcast_to(pos.reshape(s, 1), (s, LANE))
    tab = (rel_bias.T * LOG2E).astype(F32)
    dva = DIFF_V_DIM + SUM_ROWS
    smem = pl.BlockSpec(memory_space=pltpu.SMEM)
    grid_spec = pltpu.PrefetchScalarGridSpec(
        num_scalar_prefetch=10,
        grid=(DIFF_HEADS, nq),
        in_specs=[smem, smem,
                  pl.BlockSpec((DIFF_V_DIM, tq), lambda h, i, *_: (h, i)),
                  pl.BlockSpec((s, 2 * DIFF_HEAD_DIM), lambda h, i, *_: (0, DIFF_HEADS + h)),
                  pl.BlockSpec((DIFF_V_DIM, s), lambda h, i, *_: (h, 0)),
                  pl.BlockSpec((1, tq), lambda h, i, *_: (0, i)),
                  pl.BlockSpec((s, LANE), lambda h, i, *_: (0, 0)),
                  pl.BlockSpec((DIFF_V_DIM, 1), lambda h, i, *_: (0, 0))],
        out_specs=pl.BlockSpec((DIFF_V_DIM, tq), lambda h, i, *_: (h, i)),
        scratch_shapes=[pltpu.VMEM((2, DIFF_V_DIM, tq), BF16),
                        pltpu.VMEM((2 * nc, dva, CHAIN_W), F32),
                        pltpu.VMEM((2, 2 * nc, tk, CHAIN_W), F32),
                        pltpu.VMEM((2, 2 * nc, tk, CHAIN_W), BF16),
                        pltpu.VMEM((2, tk, tq), F32)],
    )
    return pl.pallas_call(
        functools.partial(_diff_attn_body, tk=tk, nk=nk, nc=nc, sk=sk, out_scale=1.0 - lam_init),
        grid_spec=grid_spec,
        out_shape=jax.ShapeDtypeStruct((DIFF_HEADS * DIFF_V_DIM, s), BF16),
        compiler_params=_cparams(("parallel", "parallel")),
        name="diff_attention",
    )(order.reshape(nq * nk), nfar, kmin, kmax, smin, smax, qmin, qmax, cmin, cmax,
      tab, lam.reshape(1).astype(F32),
      qT, qkv, vT, posq, posk,
      subln.reshape(DIFF_V_DIM, 1).astype(F32))


def _xa_body(x_ref, gpre_ref, wq_ref, kT_ref, v_ref, wo_ref, gpost_ref, gnext_ref, o_ref, h_ref):
    x = x_ref[...]
    h = (_rms(x, NORM_EPS) * gpre_ref[...]).astype(BF16)
    q = jnp.dot(h, wq_ref[...], preferred_element_type=F32) * (MEM_HEAD_DIM ** -0.5)
    q = q.astype(BF16)
    outs = []
    for hd in range(MEM_HEADS):
        sl = slice(hd * MEM_HEAD_DIM, (hd + 1) * MEM_HEAD_DIM)
        s = jnp.dot(q[:, sl], kT_ref[sl, :], preferred_element_type=F32)
        p = jnp.exp(s - jnp.max(s, axis=-1, keepdims=True))
        l = jnp.sum(p, axis=-1, keepdims=True)
        o = jnp.dot(p.astype(BF16), v_ref[:, sl], preferred_element_type=F32) / l
        outs.append(o.astype(BF16))
    o = jnp.concatenate(outs, axis=1)
    y = jnp.dot(o, wo_ref[...], preferred_element_type=F32)
    x = x + _rms(y, NORM_EPS) * gpost_ref[...]
    o_ref[...] = x
    h_ref[...] = (_rms(x, NORM_EPS) * gnext_ref[...]).astype(h_ref.dtype)


def cross_attention_sublayer(x, g_pre, w_q, kT, v, w_o, l, g_post, g_next):
    s, d = x.shape
    ts = _pick(s, 256, 8)
    row = lambda i: (i, 0)
    fixed = lambda i: (0, 0)
    return pl.pallas_call(
        _xa_body,
        grid=(s // ts,),
        in_specs=[pl.BlockSpec((ts, d), row), pl.BlockSpec((1, d), fixed),
                  pl.BlockSpec((None,) + w_q.shape[1:], lambda i: (l, 0, 0)), pl.BlockSpec(kT.shape, fixed),
                  pl.BlockSpec(v.shape, fixed), pl.BlockSpec((None,) + w_o.shape[1:], lambda i: (l, 0, 0)),
                  pl.BlockSpec((1, d), fixed), pl.BlockSpec((1, d), fixed)],
        out_specs=[pl.BlockSpec((ts, d), row), pl.BlockSpec((ts, d), row)],
        out_shape=[jax.ShapeDtypeStruct((s, d), F32), jax.ShapeDtypeStruct((s, d), BF16)],
        compiler_params=_cparams(("parallel",)),
        name="cross_attention",
    )(x, g_pre.reshape(1, d), w_q, kT, v, w_o, g_post.reshape(1, d), g_next.reshape(1, d))


def _rot_cols(w):
    half = w.shape[-1] // 2
    return jnp.concatenate([-w[..., half:], w[..., :half]], axis=-1)


def _prep_w_in(w, q_rank, kv_rank):
    a = q_rank + kv_rank
    kr = w[..., a:a + MLA_ROPE_DIM]
    z = jnp.zeros_like(kr)
    w_lat = jnp.concatenate([w[..., :a], kr, z, _rot_cols(kr), z], axis=-1)
    b = a + MLA_ROPE_DIM
    nq = DIFF_HEADS * DIFF_V_DIM
    w_q = w[..., b:b + nq] * (LOG2E * DIFF_HEAD_DIM ** -0.5)
    w_qkvd = jnp.concatenate([w_q, w[..., b + nq:b + 3 * nq]], axis=-1)
    return w_lat.astype(BF16), w_qkvd.astype(BF16), w[..., b + 3 * nq:].astype(BF16)


def _prep_w_uq(w):
    nl, r, _ = w.shape
    w4 = w.reshape(nl, r, MLA_HEADS, MLA_QK_DIM)
    nope, rope = w4[..., :MLA_NOPE_DIM], w4[..., MLA_NOPE_DIM:]
    z = jnp.zeros_like(rope)
    out = jnp.concatenate([nope, rope, z, _rot_cols(rope), z], axis=-1)
    return out.reshape(nl, r, -1).astype(BF16)


def _prep_w_ukv(w):
    nl, r, _ = w.shape
    w4 = w.reshape(nl, r, MLA_HEADS, MLA_NOPE_DIM + MLA_V_DIM)
    k_nope = w4[..., :MLA_NOPE_DIM].reshape(nl, r, -1)
    v = w4[..., MLA_NOPE_DIM:].reshape(nl, r, -1)
    return jnp.concatenate([k_nope, v], axis=-1).astype(BF16)


def _rope_tables(positions, s):
    half = MLA_ROPE_DIM // 2
    inv_freq = ROPE_BASE ** (-jnp.arange(half, dtype=F32) / half)
    ang = positions.reshape(s, 1).astype(F32) * inv_freq
    z = jnp.zeros((s, LANE - MLA_ROPE_DIM), F32)
    cos = jnp.concatenate([jnp.cos(ang), jnp.cos(ang), z], axis=1)
    sin = jnp.concatenate([jnp.sin(ang), jnp.sin(ang), z], axis=1)
    return cos, sin


def kernel(x, mem, positions, rel_bias, mix_norm_pre, mix_norm_post, w_in, mla_q_norm, mla_w_uq, mla_kv_norm, mla_w_ukv, diff_lambda, diff_subln, w_mla_branch, w_diff_branch, w_out, xa_norm_pre, xa_norm_post, xa_mem_norm, xa_w_q, xa_w_kv, xa_w_o, ffn_norm_pre, ffn_norm_post, ffn_w_in, ffn_w_out):
    b, s, d = x.shape
    assert b == 1, "kernel is written for batch 1"
    depth = w_in.shape[0]
    q_rank, kv_rank = mla_q_norm.shape[1], mla_kv_norm.shape[1]
    dff = ffn_w_out.shape[1]
    dff_pad = -(-dff // 512) * 512
    nh = DIFF_HEADS * DIFF_V_DIM

    xs = x.reshape(s, d)
    mems = mem.reshape(mem.shape[1], d)
    cos, sin = _rope_tables(positions, s)

    w_lat, w_qkvd, w_gate = _prep_w_in(w_in, q_rank, kv_rank)
    w_uq, w_ukv = _prep_w_uq(mla_w_uq), _prep_w_ukv(mla_w_ukv)
    w_mb, w_db, w_o = w_mla_branch.astype(BF16), w_diff_branch.astype(BF16), w_out.astype(BF16)
    w_xq, w_xkv, w_xo = xa_w_q.astype(BF16), xa_w_kv.astype(BF16), xa_w_o.astype(BF16)
    pad = dff_pad - dff
    w_gu = jnp.concatenate([jnp.pad(ffn_w_in[..., :dff], ((0, 0), (0, 0), (0, pad))),
                            jnp.pad(ffn_w_in[..., dff:], ((0, 0), (0, 0), (0, pad)))], axis=-1).astype(BF16)
    w_fo = jnp.pad(ffn_w_out, ((0, 0), (0, pad), (0, 0))).astype(BF16)

    h = rmsnorm(xs, mix_norm_pre[0], BF16)
    for l in range(depth):
        lat = matmul(h, w_lat, l, F32, bn=768)
        qkvd = matmul(h, w_qkvd, l, BF16)
        gates = matmul(h, w_gate, l, BF16)

        q, k, v = mla_prep(lat, mla_q_norm[l], mla_kv_norm[l], w_uq, w_ukv, l, cos, sin)
        o_mla = mla_attention(q.T, k, v.T).T

        lam_init = 0.8 - 0.6 * math.exp(-0.3 * l)
        lv = diff_lambda[l].astype(F32)
        lam = jnp.exp(jnp.sum(lv[0] * lv[1])) - jnp.exp(jnp.sum(lv[2] * lv[3])) + lam_init
        qT_d = qkvd[:, :nh].T
        vT_d = qkvd[:, 2 * nh:].T
        o_diff = diff_attention(qT_d, qkvd, vT_d, positions, rel_bias, lam, diff_subln[l], lam_init).T

        merged = gated_merge(o_mla, o_diff, w_mb, w_db, l, gates)
        y = matmul(merged, w_o, l, F32)
        xs = add_rmsnorm(xs, y, mix_norm_post[l])

        mem_n = rmsnorm(mems, xa_mem_norm[l], BF16)
        kv = matmul(mem_n, w_xkv, l, BF16)
        hm = MEM_HEADS * MEM_HEAD_DIM
        xs, h = cross_attention_sublayer(xs, xa_norm_pre[l], w_xq, kv[:, :hm].T, kv[:, hm:], w_xo, l,
                                         xa_norm_post[l], ffn_norm_pre[l])

        act = swiglu_matmul(h, w_gu, l, dff_pad)
        y = matmul(act, w_fo, l, F32, bk=2816)
        if l + 1 < depth:
            xs, h = add_rmsnorm(xs, y, ffn_norm_post[l], mix_norm_pre[l + 1])
        else:
            xs = add_rmsnorm(xs, y, ffn_norm_post[l])

    return xs.reshape(b, s, d)
```

```python
import functools
import math

import jax
import jax.numpy as jnp
from jax import lax
from jax.experimental import pallas as pl
from jax.experimental.pallas import tpu as pltpu

F32 = jnp.float32
BF16 = jnp.bfloat16

MLA_HEADS = 8
MLA_NOPE_DIM = 128
MLA_ROPE_DIM = 64
MLA_V_DIM = 128
MLA_QK_DIM = MLA_NOPE_DIM + MLA_ROPE_DIM
DIFF_HEADS = 8
DIFF_HEAD_DIM = 64
DIFF_V_DIM = 2 * DIFF_HEAD_DIM
MEM_HEADS = 4
MEM_HEAD_DIM = 128
REL_BUCKETS = 32
REL_MAX_DISTANCE = 128
ROPE_BASE = 10000.0
NORM_EPS = 1e-6
DIFF_SUBLN_EPS = 1e-5

LANE = 128
VMEM_LIMIT_BYTES = 56 * 1024 * 1024
NEG_BIG = -1e30
LOG2E = math.log2(math.e)


def _cparams(sem):
    return pltpu.CompilerParams(dimension_semantics=sem, vmem_limit_bytes=VMEM_LIMIT_BYTES)


def _pick(dim, target, align=LANE):
    if dim <= target:
        return dim
    t = (target // align) * align
    while t >= align:
        if dim % t == 0:
            return t
        t -= align
    return dim


def _sigmoid(x):
    return 1.0 / (1.0 + jnp.exp(-x))


def _rms(x, eps):
    return x * lax.rsqrt(jnp.mean(x * x, axis=-1, keepdims=True) + eps)


def _rmsnorm_body(x_ref, g_ref, o_ref, *, eps):
    x = x_ref[...].astype(F32)
    o_ref[...] = (_rms(x, eps) * g_ref[...]).astype(o_ref.dtype)


def rmsnorm(x, g, out_dtype, eps=NORM_EPS):
    m, d = x.shape
    bm = _pick(m, 256, 8)
    return pl.pallas_call(
        functools.partial(_rmsnorm_body, eps=eps),
        grid=(m // bm,),
        in_specs=[pl.BlockSpec((bm, d), lambda i: (i, 0)), pl.BlockSpec((1, d), lambda i: (0, 0))],
        out_specs=pl.BlockSpec((bm, d), lambda i: (i, 0)),
        out_shape=jax.ShapeDtypeStruct((m, d), out_dtype),
        compiler_params=_cparams(("parallel",)),
        name="rmsnorm",
    )(x, g.reshape(1, d))


def _add_rmsnorm_body(x_ref, y_ref, g_ref, *rest, eps, with_next):
    y = y_ref[...].astype(F32)
    x = x_ref[...] + _rms(y, eps) * g_ref[...]
    if with_next:
        gn_ref, o_ref, h_ref = rest
        h_ref[...] = (_rms(x, eps) * gn_ref[...]).astype(h_ref.dtype)
    else:
        (o_ref,) = rest
    o_ref[...] = x


def add_rmsnorm(x, y, g, g_next=None, eps=NORM_EPS):
    m, d = x.shape
    bm = _pick(m, 256, 8)
    row = pl.BlockSpec((bm, d), lambda i: (i, 0))
    vec = pl.BlockSpec((1, d), lambda i: (0, 0))
    with_next = g_next is not None
    args = (x, y, g.reshape(1, d)) + ((g_next.reshape(1, d),) if with_next else ())
    return pl.pallas_call(
        functools.partial(_add_rmsnorm_body, eps=eps, with_next=with_next),
        grid=(m // bm,),
        in_specs=[row, row, vec] + ([vec] if with_next else []),
        out_specs=[row, row] if with_next else row,
        out_shape=([jax.ShapeDtypeStruct((m, d), F32), jax.ShapeDtypeStruct((m, d), BF16)] if with_next
                   else jax.ShapeDtypeStruct((m, d), F32)),
        compiler_params=_cparams(("parallel",)),
        name="add_rmsnorm",
    )(*args)


def _matmul_body(a_ref, w_ref, o_ref, *scratch, nk):
    part = jnp.dot(a_ref[...], w_ref[...], preferred_element_type=F32)
    if nk == 1:
        o_ref[...] = part.astype(o_ref.dtype)
        return
    (acc_ref,) = scratch
    k = pl.program_id(2)

    @pl.when(k == 0)
    def _():
        acc_ref[...] = part

    @pl.when(k > 0)
    def _():
        acc_ref[...] += part

    @pl.when(k == nk - 1)
    def _():
        o_ref[...] = acc_ref[...].astype(o_ref.dtype)


def matmul(a, w, l, out_dtype, bm=1024, bn=1024, bk=4096):
    m, kd = a.shape
    _, _, n = w.shape
    bm, bn, bk = _pick(m, bm, 8), _pick(n, bn), _pick(kd, bk)
    nk = kd // bk
    return pl.pallas_call(
        functools.partial(_matmul_body, nk=nk),
        grid=(m // bm, n // bn, nk),
        in_specs=[pl.BlockSpec((bm, bk), lambda i, j, k: (i, k)),
                  pl.BlockSpec((None, bk, bn), lambda i, j, k: (l, k, j))],
        out_specs=pl.BlockSpec((bm, bn), lambda i, j, k: (i, j)),
        out_shape=jax.ShapeDtypeStruct((m, n), out_dtype),
        scratch_shapes=[pltpu.VMEM((bm, bn), F32)] if nk > 1 else [],
        compiler_params=_cparams(("parallel", "parallel", "arbitrary")),
        name="matmul",
    )(a, w)


def _swiglu_body(a_ref, wg_ref, wu_ref, o_ref):
    a = a_ref[...]
    g = jnp.dot(a, wg_ref[...], preferred_element_type=F32)
    u = jnp.dot(a, wu_ref[...], preferred_element_type=F32)
    o_ref[...] = (g * _sigmoid(g) * u).astype(o_ref.dtype)


def swiglu_matmul(a, w_gu, l, dff, bm=1024, bn=512):
    m, kd = a.shape
    bm, bn = _pick(m, bm, 8), _pick(dff, bn)
    nj = dff // bn
    return pl.pallas_call(
        _swiglu_body,
        grid=(m // bm, nj),
        in_specs=[pl.BlockSpec((bm, kd), lambda i, j: (i, 0)),
                  pl.BlockSpec((None, kd, bn), lambda i, j: (l, 0, j)),
                  pl.BlockSpec((None, kd, bn), lambda i, j: (l, 0, j + nj))],
        out_specs=pl.BlockSpec((bm, bn), lambda i, j: (i, j)),
        out_shape=jax.ShapeDtypeStruct((m, dff), BF16),
        compiler_params=_cparams(("parallel", "parallel")),
        name="swiglu_matmul",
    )(a, w_gu, w_gu)


def _merge_body(om_ref, od_ref, wm_ref, wd_ref, gm_ref, gd_ref, o_ref):
    ym = jnp.dot(om_ref[...], wm_ref[...], preferred_element_type=F32)
    yd = jnp.dot(od_ref[...], wd_ref[...], preferred_element_type=F32)
    sm = _sigmoid(gm_ref[...].astype(F32))
    sd = _sigmoid(gd_ref[...].astype(F32))
    o_ref[...] = (sm * ym + sd * yd).astype(o_ref.dtype)


def gated_merge(o_mla, o_diff, w_mla, w_diff, l, gates, bm=1024, bn=1024):
    m, km = o_mla.shape
    _, kd = o_diff.shape
    _, _, n = w_mla.shape
    bm, bn = _pick(m, bm, 8), _pick(n, bn)
    nj = n // bn
    return pl.pallas_call(
        _merge_body,
        grid=(m // bm, nj),
        in_specs=[pl.BlockSpec((bm, km), lambda i, j: (i, 0)),
                  pl.BlockSpec((bm, kd), lambda i, j: (i, 0)),
                  pl.BlockSpec((None, km, bn), lambda i, j: (l, 0, j)),
                  pl.BlockSpec((None, kd, bn), lambda i, j: (l, 0, j)),
                  pl.BlockSpec((bm, bn), lambda i, j: (i, j)),
                  pl.BlockSpec((bm, bn), lambda i, j: (i, j + nj))],
        out_specs=pl.BlockSpec((bm, bn), lambda i, j: (i, j)),
        out_shape=jax.ShapeDtypeStruct((m, n), BF16),
        compiler_params=_cparams(("parallel", "parallel")),
        name="gated_merge",
    )(o_mla, o_diff, w_mla, w_diff, gates, gates)


def _mla_prep_body(lat_ref, gq_ref, gkv_ref, wq_ref, wkv_ref, cos_ref, sin_ref,
                   q_ref, k_ref, v_ref, *, q_rank, kv_rank, scale):
    cos = cos_ref[...]
    sin = sin_ref[...]
    cq = lat_ref[:, :q_rank].astype(F32)
    ckv = lat_ref[:, q_rank:q_rank + kv_rank].astype(F32)
    kr = lat_ref[:, q_rank + kv_rank:q_rank + kv_rank + LANE].astype(F32)
    kr_rot = lat_ref[:, q_rank + kv_rank + LANE:q_rank + kv_rank + 2 * LANE].astype(F32)
    k_rope = (kr * cos + kr_rot * sin).astype(BF16)

    cqn = (_rms(cq, NORM_EPS) * gq_ref[...]).astype(BF16)
    qa = jnp.dot(cqn, wq_ref[...], preferred_element_type=F32)
    ckvn = (_rms(ckv, NORM_EPS) * gkv_ref[...]).astype(BF16)
    kv = jnp.dot(ckvn, wkv_ref[...], preferred_element_type=F32)

    hw = MLA_NOPE_DIM + 2 * LANE
    for h in range(MLA_HEADS):
        base = h * hw
        nope = qa[:, base:base + MLA_NOPE_DIM]
        rp = qa[:, base + MLA_NOPE_DIM:base + MLA_NOPE_DIM + LANE]
        rr = qa[:, base + MLA_NOPE_DIM + LANE:base + hw]
        q_ref[:, h * 256:h * 256 + 128] = (nope * scale).astype(BF16)
        q_ref[:, h * 256 + 128:(h + 1) * 256] = ((rp * cos + rr * sin) * scale).astype(BF16)
        k_ref[:, h * 256:h * 256 + 128] = kv[:, h * 128:(h + 1) * 128].astype(BF16)
        k_ref[:, h * 256 + 128:(h + 1) * 256] = k_rope
    v_ref[...] = kv[:, MLA_HEADS * MLA_NOPE_DIM:].astype(BF16)


def mla_prep(lat, gq, gkv, wq, wkv, l, cos, sin):
    s, lw = lat.shape
    q_rank, kv_rank = gq.shape[0], gkv.shape[0]
    ts = _pick(s, 256, 8)
    hq = MLA_HEADS * 256
    hv = MLA_HEADS * MLA_V_DIM
    row = lambda i: (i, 0)
    fixed = lambda i: (0, 0)
    return pl.pallas_call(
        functools.partial(_mla_prep_body, q_rank=q_rank, kv_rank=kv_rank, scale=LOG2E * MLA_QK_DIM ** -0.5),
        grid=(s // ts,),
        in_specs=[pl.BlockSpec((ts, lw), row),
                  pl.BlockSpec((1, q_rank), fixed), pl.BlockSpec((1, kv_rank), fixed),
                  pl.BlockSpec((None,) + wq.shape[1:], lambda i: (l, 0, 0)),
                  pl.BlockSpec((None,) + wkv.shape[1:], lambda i: (l, 0, 0)),
                  pl.BlockSpec((ts, LANE), row), pl.BlockSpec((ts, LANE), row)],
        out_specs=[pl.BlockSpec((ts, hq), row), pl.BlockSpec((ts, hq), row), pl.BlockSpec((ts, hv), row)],
        out_shape=[jax.ShapeDtypeStruct((s, hq), BF16), jax.ShapeDtypeStruct((s, hq), BF16),
                   jax.ShapeDtypeStruct((s, hv), BF16)],
        compiler_params=_cparams(("parallel",)),
        name="mla_prep",
    )(lat, gq.reshape(1, -1), gkv.reshape(1, -1), wq, wkv, cos, sin)


CHAIN_W = 256
SUM_ROWS = 16


def _softmax_stage(s_ref, p_ref, slot, n, cs, ms, bias=None):
    m_new, alpha = [], []
    for j in range(n):
        s = s_ref[slot, j]
        if bias is not None:
            s = s + bias(j)
        t = jnp.max(s, axis=0, keepdims=True)
        if cs is None:
            shift = m = jnp.maximum(ms[j], t)
        else:
            shift = jnp.maximum(ms[j] - cs, t)
            m = shift + cs
        p_ref[slot, j] = jnp.exp2(s - shift).astype(BF16)
        alpha.append(jnp.exp2(ms[j] - m))
        m_new.append(m)
    return tuple(m_new), tuple(alpha)


def _value_stage(acc_ref, p_ref, slot, n, alpha, vt):
    vt = jnp.concatenate([vt, jnp.ones((SUM_ROWS, vt.shape[1]), vt.dtype)], axis=0)
    for j in range(n):
        acc_ref[j] = alpha[j] * acc_ref[j] + jnp.dot(vt, p_ref[slot, j], preferred_element_type=F32)


def _mla_attn_body(qT_ref, k_ref, vT_ref, o_ref, acc_ref, s_ref, p_ref, *, tk, nk, nc, group_n):
    acc_ref[...] = jnp.zeros_like(acc_ref)

    def score_stage(i, slot):
        kt = k_ref[pl.ds(pl.multiple_of(i * tk, tk), tk), :]
        for c in range(nc):
            s_ref[slot, c] = jnp.dot(kt, qT_ref[:, c * CHAIN_W:(c + 1) * CHAIN_W], preferred_element_type=F32)

    def value_stage(i, slot, alpha):
        vt = vT_ref[:, pl.ds(pl.multiple_of(i * tk, tk), tk)]
        _value_stage(acc_ref, p_ref, slot, nc, alpha, vt)

    score_stage(0, 0)

    def group(g, ms):
        for u in range(group_n):
            i = g * group_n + u
            slot = u % 2
            score_stage(jnp.minimum(i + 1, nk - 1), 1 - slot)
            ms, alpha = _softmax_stage(s_ref, p_ref, slot, nc, None, ms)
            value_stage(i, slot, alpha)
        return ms

    neg = jnp.full((1, CHAIN_W), NEG_BIG, F32)
    lax.fori_loop(0, nk // group_n, group, (neg,) * nc)
    for c in range(nc):
        o = acc_ref[c, :MLA_V_DIM] / acc_ref[c, MLA_V_DIM:MLA_V_DIM + 1]
        o_ref[:, c * CHAIN_W:(c + 1) * CHAIN_W] = o.astype(o_ref.dtype)


def mla_attention(qT, k, vT, nc=8, tk=1024):
    s = k.shape[0]
    tq = _pick(s, nc * CHAIN_W, CHAIN_W)
    nc = tq // CHAIN_W
    tk = _pick(s // 2, tk)
    nk = s // tk
    group_n = 2
    assert nk % group_n == 0
    dva = MLA_V_DIM + SUM_ROWS
    return pl.pallas_call(
        functools.partial(_mla_attn_body, tk=tk, nk=nk, nc=nc, group_n=group_n),
        grid=(MLA_HEADS, s // tq),
        in_specs=[pl.BlockSpec((256, tq), lambda h, i: (h, i)),
                  pl.BlockSpec((s, 256), lambda h, i: (0, h)),
                  pl.BlockSpec((MLA_V_DIM, s), lambda h, i: (h, 0))],
        out_specs=pl.BlockSpec((MLA_V_DIM, tq), lambda h, i: (h, i)),
        out_shape=jax.ShapeDtypeStruct((MLA_HEADS * MLA_V_DIM, s), BF16),
        scratch_shapes=[pltpu.VMEM((nc, dva, CHAIN_W), F32),
                        pltpu.VMEM((2, nc, tk, CHAIN_W), F32),
                        pltpu.VMEM((2, nc, tk, CHAIN_W), BF16)],
        compiler_params=_cparams(("parallel", "parallel")),
        name="mla_attention",
    )(qT, k, vT)


def _t5_bias(rel, tabv):
    n = jnp.minimum(jnp.abs(rel), REL_MAX_DISTANCE - 1)
    e = lax.shift_right_logical(pltpu.bitcast((n * n).astype(F32), jnp.int32), 23) - 125
    bucket = jnp.where(n < REL_BUCKETS // 4, n, e) + jnp.where(rel > 0, REL_BUCKETS // 2, 0)
    return jnp.concatenate(
        [jnp.take_along_axis(tabv, bucket[:, c * LANE:(c + 1) * LANE], axis=1) for c in range(CHAIN_W // LANE)],
        axis=1)


def _diff_attn_body(order_ref, nfar_ref, kmin_ref, kmax_ref, smin_ref, smax_ref, qmin_ref, qmax_ref,
                    cmin_ref, cmax_ref,
                    tab_ref, lam_ref, qT_ref, k_ref, vT_ref, posq_ref, posk_ref, g_ref, tabv_ref,
                    o_ref, qm_ref, acc_ref, s_ref, p_ref, bias_ref, *, tk, nk, nc, sk, out_scale):
    h = pl.program_id(0)
    qi = pl.program_id(1)
    nch = 2 * nc
    qT = qT_ref[...]
    row = lax.broadcasted_iota(jnp.int32, qT.shape, 0)
    zero = jnp.zeros_like(qT)
    qm_ref[0] = jnp.where(row < DIFF_HEAD_DIM, qT, zero)
    qm_ref[1] = jnp.where(row >= DIFF_HEAD_DIM, qT, zero)
    b_after = tab_ref[h, REL_BUCKETS - 1]
    b_before = tab_ref[h, REL_BUCKETS // 2 - 1]
    q_lo = qmin_ref[qi]
    q_hi = qmax_ref[qi]
    acc_ref[...] = jnp.zeros_like(acc_ref)
    nsub = tk // sk
    nfar = nfar_ref[qi]

    def tile_at(t):
        return order_ref[qi * nk + t]

    def score_stage(i, slot):
        kt = k_ref[pl.ds(pl.multiple_of(i * tk, tk), tk), :]
        for j in range(nch):
            q = qm_ref[j % 2, :, (j // 2) * CHAIN_W:(j // 2 + 1) * CHAIN_W]
            s_ref[slot, j] = jnp.dot(kt, q, preferred_element_type=F32)

    def value_stage(i, slot, alpha):
        vt = vT_ref[:, pl.ds(pl.multiple_of(i * tk, tk), tk)]
        _value_stage(acc_ref, p_ref, slot, nch, alpha, vt)

    def far_bias(i):
        return jnp.where(kmin_ref[i] - q_hi >= REL_MAX_DISTANCE, b_after, b_before)

    @pl.when(nfar > 0)
    def _():
        score_stage(tile_at(0), 0)

    def far_tile_at(t):
        return tile_at(jnp.minimum(t, nfar - 1))

    def far_group(t0, n, ms):
        tiles = [far_tile_at(t0 + u) for u in range(n + 1)]
        for u in range(n):
            slot = u % 2
            score_stage(tiles[u + 1], 1 - slot)
            ms, alpha = _softmax_stage(s_ref, p_ref, slot, nch, far_bias(tiles[u]), ms)
            value_stage(tiles[u], slot, alpha)
        return ms

    neg = jnp.full((1, CHAIN_W), NEG_BIG, F32)
    ms = lax.fori_loop(0, nfar // 8, lambda g, ms: far_group(8 * g, 8, ms), (neg,) * nch)
    ms = lax.fori_loop(0, (nfar % 8) // 4, lambda g, ms: far_group(nfar // 8 * 8, 4, ms), ms)
    ms = lax.fori_loop(0, (nfar % 4) // 2, lambda g, ms: far_group(nfar - 2, 2, ms), ms)

    def fill_bias(i, slot):
        off = pl.multiple_of(i * tk, tk)

        def fill(n, _):
            tt, cc = n // nc, n % nc
            u = i * nsub + tt
            cq = qi * nc + cc
            blk_after = smin_ref[u] - cmax_ref[cq] >= REL_MAX_DISTANCE
            blk_before = smax_ref[u] - cmin_ref[cq] <= -REL_MAX_DISTANCE
            roff = pl.multiple_of(tt * sk, sk)
            coff = pl.multiple_of(cc * CHAIN_W, CHAIN_W)

            @pl.when(jnp.logical_or(blk_after, blk_before))
            def _():
                c = jnp.where(blk_after, b_after, b_before)
                bias_ref[slot, pl.ds(roff, sk), pl.ds(coff, CHAIN_W)] = jnp.full((sk, CHAIN_W), c, F32)

            @pl.when(jnp.logical_not(jnp.logical_or(blk_after, blk_before)))
            def _():
                pk = posk_ref[pl.ds(pl.multiple_of(off + roff, sk), sk), :]
                pk = jnp.concatenate([pk] * (CHAIN_W // LANE), axis=1)
                rel = pk - posq_ref[:, pl.ds(coff, CHAIN_W)]
                tabv = jnp.broadcast_to(tabv_ref[0], (sk, LANE))
                bias_ref[slot, pl.ds(roff, sk), pl.ds(coff, CHAIN_W)] = _t5_bias(rel, tabv)

            return 0

        lax.fori_loop(0, nsub * nc, fill, 0)

    def near_stage(i, slot, ms):
        ms, alpha = _softmax_stage(s_ref, p_ref, slot, nch, None, ms,
                                   bias=lambda j: bias_ref[slot, :, (j // 2) * CHAIN_W:(j // 2 + 1) * CHAIN_W])
        value_stage(i, slot, alpha)
        return ms

    def near_pair(g, ms):
        ia, ib = tile_at(nfar + 2 * g), tile_at(nfar + 2 * g + 1)
        fill_bias(ia, 0)
        fill_bias(ib, 1)
        score_stage(ia, 0)
        score_stage(ib, 1)
        ms = near_stage(ia, 0, ms)
        return near_stage(ib, 1, ms)

    def near_single(g, ms):
        i = tile_at(nk - 1)
        fill_bias(i, 0)
        score_stage(i, 0)
        return near_stage(i, 0, ms)

    ms = lax.fori_loop(0, (nk - nfar) // 2, near_pair, ms)
    lax.fori_loop(0, (nk - nfar) % 2, near_single, ms)

    lam = lam_ref[0]
    dv = DIFF_V_DIM
    for c in range(nc):
        o1 = acc_ref[2 * c, :dv] / acc_ref[2 * c, dv:dv + 1]
        o2 = acc_ref[2 * c + 1, :dv] / acc_ref[2 * c + 1, dv:dv + 1]
        o = o1 - lam * o2
        msq = jnp.mean(o * o, axis=0, keepdims=True)
        o = o * lax.rsqrt(msq + DIFF_SUBLN_EPS) * g_ref[...]
        o_ref[:, c * CHAIN_W:(c + 1) * CHAIN_W] = (o * out_scale).astype(o_ref.dtype)


def diff_attention(qT, qkv, vT, positions, rel_bias, lam, subln, lam_init, nc=4, tk=512, sk=128):
    s = qkv.shape[0]
    tq = _pick(s, nc * CHAIN_W, CHAIN_W)
    nc = tq // CHAIN_W
    tk = _pick(s, tk)
    sk = _pick(tk, sk)
    nk, nq = s // tk, s // tq
    pos = positions.reshape(s).astype(jnp.int32)
    kmin = pos.reshape(nk, tk).min(axis=1)
    kmax = pos.reshape(nk, tk).max(axis=1)
    smin = pos.reshape(s // sk, sk).min(axis=1)
    smax = pos.reshape(s // sk, sk).max(axis=1)
    qmin = pos.reshape(nq, tq).min(axis=1)
    qmax = pos.reshape(nq, tq).max(axis=1)
    cmin = pos.reshape(s // CHAIN_W, CHAIN_W).min(axis=1)
    cmax = pos.reshape(s // CHAIN_W, CHAIN_W).max(axis=1)
    far = jnp.logical_or(kmin[None, :] - qmax[:, None] >= REL_MAX_DISTANCE,
                         kmax[None, :] - qmin[:, None] <= -REL_MAX_DISTANCE)
    order = jnp.argsort(jnp.logical_not(far), axis=1, stable=True).astype(jnp.int32)
    nfar = far.sum(axis=1).astype(jnp.int32)
    nfar = nfar - nfar % 2
    posq = pos.reshape(1, s)
    posk = jnp.broadcast_to(pos.reshape(s, 1), (s, LANE))
    tab = (rel_bias.T * LOG2E).astype(F32)
    dva = DIFF_V_DIM + SUM_ROWS
    smem = pl.BlockSpec(memory_space=pltpu.SMEM)
    grid_spec = pltpu.PrefetchScalarGridSpec(
        num_scalar_prefetch=10,
        grid=(DIFF_HEADS, nq),
        in_specs=[smem, smem,
                  pl.BlockSpec((DIFF_V_DIM, tq), lambda h, i, *_: (h, i)),
                  pl.BlockSpec((s, 2 * DIFF_HEAD_DIM), lambda h, i, *_: (0, DIFF_HEADS + h)),
                  pl.BlockSpec((DIFF_V_DIM, s), lambda h, i, *_: (h, 0)),
                  pl.BlockSpec((1, tq), lambda h, i, *_: (0, i)),
                  pl.BlockSpec((s, LANE), lambda h, i, *_: (0, 0)),
                  pl.BlockSpec((DIFF_V_DIM, 1), lambda h, i, *_: (0, 0)),
                  pl.BlockSpec((1, 1, LANE), lambda h, i, *_: (h, 0, 0))],
        out_specs=pl.BlockSpec((DIFF_V_DIM, tq), lambda h, i, *_: (h, i)),
        scratch_shapes=[pltpu.VMEM((2, DIFF_V_DIM, tq), BF16),
                        pltpu.VMEM((2 * nc, dva, CHAIN_W), F32),
                        pltpu.VMEM((2, 2 * nc, tk, CHAIN_W), F32),
                        pltpu.VMEM((2, 2 * nc, tk, CHAIN_W), BF16),
                        pltpu.VMEM((2, tk, tq), F32)],
    )
    return pl.pallas_call(
        functools.partial(_diff_attn_body, tk=tk, nk=nk, nc=nc, sk=sk, out_scale=1.0 - lam_init),
        grid_spec=grid_spec,
        out_shape=jax.ShapeDtypeStruct((DIFF_HEADS * DIFF_V_DIM, s), BF16),
        compiler_params=_cparams(("parallel", "parallel")),
        name="diff_attention",
    )(order.reshape(nq * nk), nfar, kmin, kmax, smin, smax, qmin, qmax, cmin, cmax,
      tab, lam.reshape(1).astype(F32),
      qT, qkv, vT, posq, posk,
      subln.reshape(DIFF_V_DIM, 1).astype(F32),
      jnp.pad(tab, ((0, 0), (0, LANE - REL_BUCKETS))).reshape(DIFF_HEADS, 1, LANE))


def _xa_body(x_ref, gpre_ref, wq_ref, kT_ref, v_ref, wo_ref, gpost_ref, gnext_ref, o_ref, h_ref):
    x = x_ref[...]
    h = (_rms(x, NORM_EPS) * gpre_ref[...]).astype(BF16)
    q = jnp.dot(h, wq_ref[...], preferred_element_type=F32) * (MEM_HEAD_DIM ** -0.5)
    q = q.astype(BF16)
    outs = []
    for hd in range(MEM_HEADS):
        sl = slice(hd * MEM_HEAD_DIM, (hd + 1) * MEM_HEAD_DIM)
        s = jnp.dot(q[:, sl], kT_ref[sl, :], preferred_element_type=F32)
        p = jnp.exp(s - jnp.max(s, axis=-1, keepdims=True))
        l = jnp.sum(p, axis=-1, keepdims=True)
        o = jnp.dot(p.astype(BF16), v_ref[:, sl], preferred_element_type=F32) / l
        outs.append(o.astype(BF16))
    o = jnp.concatenate(outs, axis=1)
    y = jnp.dot(o, wo_ref[...], preferred_element_type=F32)
    x = x + _rms(y, NORM_EPS) * gpost_ref[...]
    o_ref[...] = x
    h_ref[...] = (_rms(x, NORM_EPS) * gnext_ref[...]).astype(h_ref.dtype)


def cross_attention_sublayer(x, g_pre, w_q, kT, v, w_o, l, g_post, g_next):
    s, d = x.shape
    ts = _pick(s, 256, 8)
    row = lambda i: (i, 0)
    fixed = lambda i: (0, 0)
    return pl.pallas_call(
        _xa_body,
        grid=(s // ts,),
        in_specs=[pl.BlockSpec((ts, d), row), pl.BlockSpec((1, d), fixed),
                  pl.BlockSpec((None,) + w_q.shape[1:], lambda i: (l, 0, 0)), pl.BlockSpec(kT.shape, fixed),
                  pl.BlockSpec(v.shape, fixed), pl.BlockSpec((None,) + w_o.shape[1:], lambda i: (l, 0, 0)),
                  pl.BlockSpec((1, d), fixed), pl.BlockSpec((1, d), fixed)],
        out_specs=[pl.BlockSpec((ts, d), row), pl.BlockSpec((ts, d), row)],
        out_shape=[jax.ShapeDtypeStruct((s, d), F32), jax.ShapeDtypeStruct((s, d), BF16)],
        compiler_params=_cparams(("parallel",)),
        name="cross_attention",
    )(x, g_pre.reshape(1, d), w_q, kT, v, w_o, g_post.reshape(1, d), g_next.reshape(1, d))


def _rot_cols(w):
    half = w.shape[-1] // 2
    return jnp.concatenate([-w[..., half:], w[..., :half]], axis=-1)


def _prep_w_in(w, q_rank, kv_rank):
    a = q_rank + kv_rank
    kr = w[..., a:a + MLA_ROPE_DIM]
    z = jnp.zeros_like(kr)
    w_lat = jnp.concatenate([w[..., :a], kr, z, _rot_cols(kr), z], axis=-1)
    b = a + MLA_ROPE_DIM
    nq = DIFF_HEADS * DIFF_V_DIM
    w_q = w[..., b:b + nq] * (LOG2E * DIFF_HEAD_DIM ** -0.5)
    w_qkvd = jnp.concatenate([w_q, w[..., b + nq:b + 3 * nq]], axis=-1)
    return w_lat.astype(BF16), w_qkvd.astype(BF16), w[..., b + 3 * nq:].astype(BF16)


def _prep_w_uq(w):
    nl, r, _ = w.shape
    w4 = w.reshape(nl, r, MLA_HEADS, MLA_QK_DIM)
    nope, rope = w4[..., :MLA_NOPE_DIM], w4[..., MLA_NOPE_DIM:]
    z = jnp.zeros_like(rope)
    out = jnp.concatenate([nope, rope, z, _rot_cols(rope), z], axis=-1)
    return out.reshape(nl, r, -1).astype(BF16)


def _prep_w_ukv(w):
    nl, r, _ = w.shape
    w4 = w.reshape(nl, r, MLA_HEADS, MLA_NOPE_DIM + MLA_V_DIM)
    k_nope = w4[..., :MLA_NOPE_DIM].reshape(nl, r, -1)
    v = w4[..., MLA_NOPE_DIM:].reshape(nl, r, -1)
    return jnp.concatenate([k_nope, v], axis=-1).astype(BF16)


def _rope_tables(positions, s):
    half = MLA_ROPE_DIM // 2
    inv_freq = ROPE_BASE ** (-jnp.arange(half, dtype=F32) / half)
    ang = positions.reshape(s, 1).astype(F32) * inv_freq
    z = jnp.zeros((s, LANE - MLA_ROPE_DIM), F32)
    cos = jnp.concatenate([jnp.cos(ang), jnp.cos(ang), z], axis=1)
    sin = jnp.concatenate([jnp.sin(ang), jnp.sin(ang), z], axis=1)
    return cos, sin


def kernel(x, mem, positions, rel_bias, mix_norm_pre, mix_norm_post, w_in, mla_q_norm, mla_w_uq, mla_kv_norm, mla_w_ukv, diff_lambda, diff_subln, w_mla_branch, w_diff_branch, w_out, xa_norm_pre, xa_norm_post, xa_mem_norm, xa_w_q, xa_w_kv, xa_w_o, ffn_norm_pre, ffn_norm_post, ffn_w_in, ffn_w_out):
    b, s, d = x.shape
    assert b == 1, "kernel is written for batch 1"
    depth = w_in.shape[0]
    q_rank, kv_rank = mla_q_norm.shape[1], mla_kv_norm.shape[1]
    dff = ffn_w_out.shape[1]
    dff_pad = -(-dff // 512) * 512
    nh = DIFF_HEADS * DIFF_V_DIM

    xs = x.reshape(s, d)
    mems = mem.reshape(mem.shape[1], d)
    cos, sin = _rope_tables(positions, s)

    w_lat, w_qkvd, w_gate = _prep_w_in(w_in, q_rank, kv_rank)
    w_uq, w_ukv = _prep_w_uq(mla_w_uq), _prep_w_ukv(mla_w_ukv)
    w_mb, w_db, w_o = w_mla_branch.astype(BF16), w_diff_branch.astype(BF16), w_out.astype(BF16)
    w_xq, w_xkv, w_xo = xa_w_q.astype(BF16), xa_w_kv.astype(BF16), xa_w_o.astype(BF16)
    pad = dff_pad - dff
    w_gu = jnp.concatenate([jnp.pad(ffn_w_in[..., :dff], ((0, 0), (0, 0), (0, pad))),
                            jnp.pad(ffn_w_in[..., dff:], ((0, 0), (0, 0), (0, pad)))], axis=-1).astype(BF16)
    w_fo = jnp.pad(ffn_w_out, ((0, 0), (0, pad), (0, 0))).astype(BF16)

    h = rmsnorm(xs, mix_norm_pre[0], BF16)
    for l in range(depth):
        lat = matmul(h, w_lat, l, F32, bn=768)
        qkvd = matmul(h, w_qkvd, l, BF16)
        gates = matmul(h, w_gate, l, BF16)

        q, k, v = mla_prep(lat, mla_q_norm[l], mla_kv_norm[l], w_uq, w_ukv, l, cos, sin)
        o_mla = mla_attention(q.T, k, v.T).T

        lam_init = 0.8 - 0.6 * math.exp(-0.3 * l)
        lv = diff_lambda[l].astype(F32)
        lam = jnp.exp(jnp.sum(lv[0] * lv[1])) - jnp.exp(jnp.sum(lv[2] * lv[3])) + lam_init
        qT_d = qkvd[:, :nh].T
        vT_d = qkvd[:, 2 * nh:].T
        o_diff = diff_attention(qT_d, qkvd, vT_d, positions, rel_bias, lam, diff_subln[l], lam_init).T

        merged = gated_merge(o_mla, o_diff, w_mb, w_db, l, gates)
        y = matmul(merged, w_o, l, F32)
        xs = add_rmsnorm(xs, y, mix_norm_post[l])

        mem_n = rmsnorm(mems, xa_mem_norm[l], BF16)
        kv = matmul(mem_n, w_xkv, l, BF16)
        hm = MEM_HEADS * MEM_HEAD_DIM
        xs, h = cross_attention_sublayer(xs, xa_norm_pre[l], w_xq, kv[:, :hm].T, kv[:, hm:], w_xo, l,
                                         xa_norm_post[l], ffn_norm_pre[l])

        act = swiglu_matmul(h, w_gu, l, dff_pad)
        y = matmul(act, w_fo, l, F32, bk=2816)
        if l + 1 < depth:
            xs, h = add_rmsnorm(xs, y, ffn_norm_post[l], mix_norm_pre[l + 1])
        else:
            xs = add_rmsnorm(xs, y, ffn_norm_post[l])

    return xs.reshape(b, s, d)
```

```python
import functools
import math

import jax
import jax.numpy as jnp
from jax import lax
from jax.experimental import pallas as pl
from jax.experimental.pallas import tpu as pltpu

F32 = jnp.float32
BF16 = jnp.bfloat16

MLA_HEADS = 8
MLA_NOPE_DIM = 128
MLA_ROPE_DIM = 64
MLA_V_DIM = 128
MLA_QK_DIM = MLA_NOPE_DIM + MLA_ROPE_DIM
DIFF_HEADS = 8
DIFF_HEAD_DIM = 64
DIFF_V_DIM = 2 * DIFF_HEAD_DIM
MEM_HEADS = 4
MEM_HEAD_DIM = 128
REL_BUCKETS = 32
REL_MAX_DISTANCE = 128
ROPE_BASE = 10000.0
NORM_EPS = 1e-6
DIFF_SUBLN_EPS = 1e-5

LANE = 128
VMEM_LIMIT_BYTES = 56 * 1024 * 1024
NEG_BIG = -1e30
LOG2E = math.log2(math.e)


def _cparams(sem):
    return pltpu.CompilerParams(dimension_semantics=sem, vmem_limit_bytes=VMEM_LIMIT_BYTES)


def _pick(dim, target, align=LANE):
    if dim <= target:
        return dim
    t = (target // align) * align
    while t >= align:
        if dim % t == 0:
            return t
        t -= align
    return dim


def _sigmoid(x):
    return 1.0 / (1.0 + jnp.exp(-x))


def _rms(x, eps):
    return x * lax.rsqrt(jnp.mean(x * x, axis=-1, keepdims=True) + eps)


def _rmsnorm_body(x_ref, g_ref, o_ref, *, eps):
    x = x_ref[...].astype(F32)
    o_ref[...] = (_rms(x, eps) * g_ref[...]).astype(o_ref.dtype)


def rmsnorm(x, g, out_dtype, eps=NORM_EPS):
    m, d = x.shape
    bm = _pick(m, 256, 8)
    return pl.pallas_call(
        functools.partial(_rmsnorm_body, eps=eps),
        grid=(m // bm,),
        in_specs=[pl.BlockSpec((bm, d), lambda i: (i, 0)), pl.BlockSpec((1, d), lambda i: (0, 0))],
        out_specs=pl.BlockSpec((bm, d), lambda i: (i, 0)),
        out_shape=jax.ShapeDtypeStruct((m, d), out_dtype),
        compiler_params=_cparams(("parallel",)),
        name="rmsnorm",
    )(x, g.reshape(1, d))


def _add_rmsnorm_body(x_ref, y_ref, g_ref, *rest, eps, with_next):
    y = y_ref[...].astype(F32)
    x = x_ref[...] + _rms(y, eps) * g_ref[...]
    if with_next:
        gn_ref, o_ref, h_ref = rest
        h_ref[...] = (_rms(x, eps) * gn_ref[...]).astype(h_ref.dtype)
    else:
        (o_ref,) = rest
    o_ref[...] = x


def add_rmsnorm(x, y, g, g_next=None, eps=NORM_EPS):
    m, d = x.shape
    bm = _pick(m, 256, 8)
    row = pl.BlockSpec((bm, d), lambda i: (i, 0))
    vec = pl.BlockSpec((1, d), lambda i: (0, 0))
    with_next = g_next is not None
    args = (x, y, g.reshape(1, d)) + ((g_next.reshape(1, d),) if with_next else ())
    return pl.pallas_call(
        functools.partial(_add_rmsnorm_body, eps=eps, with_next=with_next),
        grid=(m // bm,),
        in_specs=[row, row, vec] + ([vec] if with_next else []),
        out_specs=[row, row] if with_next else row,
        out_shape=([jax.ShapeDtypeStruct((m, d), F32), jax.ShapeDtypeStruct((m, d), BF16)] if with_next
                   else jax.ShapeDtypeStruct((m, d), F32)),
        compiler_params=_cparams(("parallel",)),
        name="add_rmsnorm",
    )(*args)


def _matmul_body(a_ref, w_ref, o_ref, *scratch, nk):
    part = jnp.dot(a_ref[...], w_ref[...], preferred_element_type=F32)
    if nk == 1:
        o_ref[...] = part.astype(o_ref.dtype)
        return
    (acc_ref,) = scratch
    k = pl.program_id(2)

    @pl.when(k == 0)
    def _():
        acc_ref[...] = part

    @pl.when(k > 0)
    def _():
        acc_ref[...] += part

    @pl.when(k == nk - 1)
    def _():
        o_ref[...] = acc_ref[...].astype(o_ref.dtype)


def matmul(a, w, l, out_dtype, bm=1024, bn=1024, bk=4096):
    m, kd = a.shape
    _, _, n = w.shape
    bm, bn, bk = _pick(m, bm, 8), _pick(n, bn), _pick(kd, bk)
    nk = kd // bk
    return pl.pallas_call(
        functools.partial(_matmul_body, nk=nk),
        grid=(m // bm, n // bn, nk),
        in_specs=[pl.BlockSpec((bm, bk), lambda i, j, k: (i, k)),
                  pl.BlockSpec((None, bk, bn), lambda i, j, k: (l, k, j))],
        out_specs=pl.BlockSpec((bm, bn), lambda i, j, k: (i, j)),
        out_shape=jax.ShapeDtypeStruct((m, n), out_dtype),
        scratch_shapes=[pltpu.VMEM((bm, bn), F32)] if nk > 1 else [],
        compiler_params=_cparams(("parallel", "parallel", "arbitrary")),
        name="matmul",
    )(a, w)


def _swiglu_body(a_ref, wg_ref, wu_ref, o_ref):
    a = a_ref[...]
    g = jnp.dot(a, wg_ref[...], preferred_element_type=F32)
    u = jnp.dot(a, wu_ref[...], preferred_element_type=F32)
    o_ref[...] = (g * _sigmoid(g) * u).astype(o_ref.dtype)


def swiglu_matmul(a, w_g, w_u, l, bm=1024, bn=512):
    dff = w_g.shape[2]
    m, kd = a.shape
    bm, bn = _pick(m, bm, 8), _pick(dff, bn)
    return pl.pallas_call(
        _swiglu_body,
        grid=(m // bm, dff // bn),
        in_specs=[pl.BlockSpec((bm, kd), lambda i, j: (i, 0)),
                  pl.BlockSpec((None, kd, bn), lambda i, j: (l, 0, j)),
                  pl.BlockSpec((None, kd, bn), lambda i, j: (l, 0, j))],
        out_specs=pl.BlockSpec((bm, bn), lambda i, j: (i, j)),
        out_shape=jax.ShapeDtypeStruct((m, dff), BF16),
        compiler_params=_cparams(("parallel", "parallel")),
        name="swiglu_matmul",
    )(a, w_g, w_u)


def _merge_body(om_ref, od_ref, wm_ref, wd_ref, gm_ref, gd_ref, o_ref):
    ym = jnp.dot(om_ref[...], wm_ref[...], preferred_element_type=F32)
    yd = jnp.dot(od_ref[...], wd_ref[...], preferred_element_type=F32)
    sm = _sigmoid(gm_ref[...].astype(F32))
    sd = _sigmoid(gd_ref[...].astype(F32))
    o_ref[...] = (sm * ym + sd * yd).astype(o_ref.dtype)


def gated_merge(o_mla, o_diff, w_mla, w_diff, l, gates, bm=1024, bn=1024):
    m, km = o_mla.shape
    _, kd = o_diff.shape
    _, _, n = w_mla.shape
    bm, bn = _pick(m, bm, 8), _pick(n, bn)
    nj = n // bn
    return pl.pallas_call(
        _merge_body,
        grid=(m // bm, nj),
        in_specs=[pl.BlockSpec((bm, km), lambda i, j: (i, 0)),
                  pl.BlockSpec((bm, kd), lambda i, j: (i, 0)),
                  pl.BlockSpec((None, km, bn), lambda i, j: (l, 0, j)),
                  pl.BlockSpec((None, kd, bn), lambda i, j: (l, 0, j)),
                  pl.BlockSpec((bm, bn), lambda i, j: (i, j)),
                  pl.BlockSpec((bm, bn), lambda i, j: (i, j + nj))],
        out_specs=pl.BlockSpec((bm, bn), lambda i, j: (i, j)),
        out_shape=jax.ShapeDtypeStruct((m, n), BF16),
        compiler_params=_cparams(("parallel", "parallel")),
        name="gated_merge",
    )(o_mla, o_diff, w_mla, w_diff, gates, gates)


def _mla_prep_body(lat_ref, gq_ref, gkv_ref, wq_ref, wkv_ref, cos_ref, sin_ref,
                   q_ref, k_ref, v_ref, *, q_rank, kv_rank, scale):
    cos = cos_ref[...]
    sin = sin_ref[...]
    cq = lat_ref[:, :q_rank].astype(F32)
    ckv = lat_ref[:, q_rank:q_rank + kv_rank].astype(F32)
    kr = lat_ref[:, q_rank + kv_rank:q_rank + kv_rank + LANE].astype(F32)
    kr_rot = lat_ref[:, q_rank + kv_rank + LANE:q_rank + kv_rank + 2 * LANE].astype(F32)
    k_rope = (kr * cos + kr_rot * sin).astype(BF16)

    cqn = (_rms(cq, NORM_EPS) * gq_ref[...]).astype(BF16)
    qa = jnp.dot(cqn, wq_ref[...], preferred_element_type=F32)
    ckvn = (_rms(ckv, NORM_EPS) * gkv_ref[...]).astype(BF16)
    kv = jnp.dot(ckvn, wkv_ref[...], preferred_element_type=F32)

    hw = MLA_NOPE_DIM + 2 * LANE
    for h in range(MLA_HEADS):
        base = h * hw
        nope = qa[:, base:base + MLA_NOPE_DIM]
        rp = qa[:, base + MLA_NOPE_DIM:base + MLA_NOPE_DIM + LANE]
        rr = qa[:, base + MLA_NOPE_DIM + LANE:base + hw]
        q_ref[:, h * 256:h * 256 + 128] = (nope * scale).astype(BF16)
        q_ref[:, h * 256 + 128:(h + 1) * 256] = ((rp * cos + rr * sin) * scale).astype(BF16)
        k_ref[:, h * 256:h * 256 + 128] = kv[:, h * 128:(h + 1) * 128].astype(BF16)
        k_ref[:, h * 256 + 128:(h + 1) * 256] = k_rope
    v_ref[...] = kv[:, MLA_HEADS * MLA_NOPE_DIM:].astype(BF16)


def mla_prep(lat, gq, gkv, wq, wkv, l, cos, sin):
    s, lw = lat.shape
    q_rank, kv_rank = gq.shape[0], gkv.shape[0]
    ts = _pick(s, 256, 8)
    hq = MLA_HEADS * 256
    hv = MLA_HEADS * MLA_V_DIM
    row = lambda i: (i, 0)
    fixed = lambda i: (0, 0)
    return pl.pallas_call(
        functools.partial(_mla_prep_body, q_rank=q_rank, kv_rank=kv_rank, scale=LOG2E * MLA_QK_DIM ** -0.5),
        grid=(s // ts,),
        in_specs=[pl.BlockSpec((ts, lw), row),
                  pl.BlockSpec((1, q_rank), fixed), pl.BlockSpec((1, kv_rank), fixed),
                  pl.BlockSpec((None,) + wq.shape[1:], lambda i: (l, 0, 0)),
                  pl.BlockSpec((None,) + wkv.shape[1:], lambda i: (l, 0, 0)),
                  pl.BlockSpec((ts, LANE), row), pl.BlockSpec((ts, LANE), row)],
        out_specs=[pl.BlockSpec((ts, hq), row), pl.BlockSpec((ts, hq), row), pl.BlockSpec((ts, hv), row)],
        out_shape=[jax.ShapeDtypeStruct((s, hq), BF16), jax.ShapeDtypeStruct((s, hq), BF16),
                   jax.ShapeDtypeStruct((s, hv), BF16)],
        compiler_params=_cparams(("parallel",)),
        name="mla_prep",
    )(lat, gq.reshape(1, -1), gkv.reshape(1, -1), wq, wkv, cos, sin)


CHAIN_W = 256
SUM_ROWS = 16


def _softmax_stage(s_ref, p_ref, slot, n, cs, ms, bias=None):
    m_new, alpha = [], []
    for j in range(n):
        s = s_ref[slot, j]
        if bias is not None:
            s = s + bias(j)
        t = jnp.max(s, axis=0, keepdims=True)
        if cs is None:
            shift = m = jnp.maximum(ms[j], t)
        else:
            shift = jnp.maximum(ms[j] - cs, t)
            m = shift + cs
        p_ref[slot, j] = jnp.exp2(s - shift).astype(BF16)
        alpha.append(jnp.exp2(ms[j] - m))
        m_new.append(m)
    return tuple(m_new), tuple(alpha)


def _value_stage(acc_ref, p_ref, slot, n, alpha, vt):
    vt = jnp.concatenate([vt, jnp.ones((SUM_ROWS, vt.shape[1]), vt.dtype)], axis=0)
    for j in range(n):
        acc_ref[j] = alpha[j] * acc_ref[j] + jnp.dot(vt, p_ref[slot, j], preferred_element_type=F32)


def _mla_attn_body(qT_ref, k_ref, vT_ref, o_ref, acc_ref, s_ref, p_ref, *, tk, nk, nc, group_n):
    acc_ref[...] = jnp.zeros_like(acc_ref)

    def score_stage(i, slot):
        kt = k_ref[pl.ds(pl.multiple_of(i * tk, tk), tk), :]
        for c in range(nc):
            s_ref[slot, c] = jnp.dot(kt, qT_ref[:, c * CHAIN_W:(c + 1) * CHAIN_W], preferred_element_type=F32)

    def value_stage(i, slot, alpha):
        vt = vT_ref[:, pl.ds(pl.multiple_of(i * tk, tk), tk)]
        _value_stage(acc_ref, p_ref, slot, nc, alpha, vt)

    score_stage(0, 0)

    def group(g, ms):
        for u in range(group_n):
            i = g * group_n + u
            slot = u % 2
            score_stage(jnp.minimum(i + 1, nk - 1), 1 - slot)
            ms, alpha = _softmax_stage(s_ref, p_ref, slot, nc, None, ms)
            value_stage(i, slot, alpha)
        return ms

    neg = jnp.full((1, CHAIN_W), NEG_BIG, F32)
    lax.fori_loop(0, nk // group_n, group, (neg,) * nc)
    for c in range(nc):
        o = acc_ref[c, :MLA_V_DIM] / acc_ref[c, MLA_V_DIM:MLA_V_DIM + 1]
        o_ref[:, c * CHAIN_W:(c + 1) * CHAIN_W] = o.astype(o_ref.dtype)


def mla_attention(qT, k, vT, nc=8, tk=1024):
    s = k.shape[0]
    tq = _pick(s, nc * CHAIN_W, CHAIN_W)
    nc = tq // CHAIN_W
    tk = _pick(s // 2, tk)
    nk = s // tk
    group_n = 2
    assert nk % group_n == 0
    dva = MLA_V_DIM + SUM_ROWS
    return pl.pallas_call(
        functools.partial(_mla_attn_body, tk=tk, nk=nk, nc=nc, group_n=group_n),
        grid=(MLA_HEADS, s // tq),
        in_specs=[pl.BlockSpec((256, tq), lambda h, i: (h, i)),
                  pl.BlockSpec((s, 256), lambda h, i: (0, h)),
                  pl.BlockSpec((MLA_V_DIM, s), lambda h, i: (h, 0))],
        out_specs=pl.BlockSpec((MLA_V_DIM, tq), lambda h, i: (h, i)),
        out_shape=jax.ShapeDtypeStruct((MLA_HEADS * MLA_V_DIM, s), BF16),
        scratch_shapes=[pltpu.VMEM((nc, dva, CHAIN_W), F32),
                        pltpu.VMEM((2, nc, tk, CHAIN_W), F32),
                        pltpu.VMEM((2, nc, tk, CHAIN_W), BF16)],
        compiler_params=_cparams(("parallel", "parallel")),
        name="mla_attention",
    )(qT, k, vT)


T5_BUCKET_EDGES = (0, 1, 2, 3, 4, 5, 6, 7, 8, 12, 16, 23, 32, 46, 64, 91)


def _t5_bias(rel, tab_ref, h):
    nb = REL_BUCKETS // 2
    pos = rel > 0
    n = jnp.abs(rel)
    v = jnp.where(pos, tab_ref[h, nb], tab_ref[h, 0])
    for b in range(1, nb):
        leaf = jnp.where(pos, tab_ref[h, nb + b], tab_ref[h, b])
        v = jnp.where(n >= T5_BUCKET_EDGES[b], leaf, v)
    return v


def _diff_attn_body(order_ref, nfar_ref, kmin_ref, kmax_ref, smin_ref, smax_ref, qmin_ref, qmax_ref,
                    cmin_ref, cmax_ref,
                    tab_ref, lam_ref, qT_ref, k_ref, vT_ref, posq_ref, posk_ref, g_ref,
                    o_ref, qm_ref, acc_ref, s_ref, p_ref, bias_ref, *, tk, nk, nc, sk, out_scale):
    h = pl.program_id(0)
    qi = pl.program_id(1)
    nch = 2 * nc
    qT = qT_ref[...]
    row = lax.broadcasted_iota(jnp.int32, qT.shape, 0)
    zero = jnp.zeros_like(qT)
    qm_ref[0] = jnp.where(row < DIFF_HEAD_DIM, qT, zero)
    qm_ref[1] = jnp.where(row >= DIFF_HEAD_DIM, qT, zero)
    b_after = tab_ref[h, REL_BUCKETS - 1]
    b_before = tab_ref[h, REL_BUCKETS // 2 - 1]
    q_lo = qmin_ref[qi]
    q_hi = qmax_ref[qi]
    acc_ref[...] = jnp.zeros_like(acc_ref)
    nsub = tk // sk
    nfar = nfar_ref[qi]

    def tile_at(t):
        return order_ref[qi * nk + t]

    def score_stage(i, slot):
        kt = k_ref[pl.ds(pl.multiple_of(i * tk, tk), tk), :]
        for j in range(nch):
            q = qm_ref[j % 2, :, (j // 2) * CHAIN_W:(j // 2 + 1) * CHAIN_W]
            s_ref[slot, j] = jnp.dot(kt, q, preferred_element_type=F32)

    def value_stage(i, slot, alpha):
        vt = vT_ref[:, pl.ds(pl.multiple_of(i * tk, tk), tk)]
        _value_stage(acc_ref, p_ref, slot, nch, alpha, vt)

    def far_bias(i):
        return jnp.where(kmin_ref[i] - q_hi >= REL_MAX_DISTANCE, b_after, b_before)

    @pl.when(nfar > 0)
    def _():
        score_stage(tile_at(0), 0)

    def far_tile_at(t):
        return tile_at(jnp.minimum(t, nfar - 1))

    def far_group(t0, n, ms):
        tiles = [far_tile_at(t0 + u) for u in range(n + 1)]
        for u in range(n):
            slot = u % 2
            score_stage(tiles[u + 1], 1 - slot)
            ms, alpha = _softmax_stage(s_ref, p_ref, slot, nch, far_bias(tiles[u]), ms)
            value_stage(tiles[u], slot, alpha)
        return ms

    neg = jnp.full((1, CHAIN_W), NEG_BIG, F32)
    ms = lax.fori_loop(0, nfar // 8, lambda g, ms: far_group(8 * g, 8, ms), (neg,) * nch)
    ms = lax.fori_loop(0, (nfar % 8) // 4, lambda g, ms: far_group(nfar // 8 * 8, 4, ms), ms)
    ms = lax.fori_loop(0, (nfar % 4) // 2, lambda g, ms: far_group(nfar - 2, 2, ms), ms)

    def fill_bias(i, slot):
        off = pl.multiple_of(i * tk, tk)

        def fill(n, _):
            tt, cc = n // nc, n % nc
            u = i * nsub + tt
            cq = qi * nc + cc
            blk_after = smin_ref[u] - cmax_ref[cq] >= REL_MAX_DISTANCE
            blk_before = smax_ref[u] - cmin_ref[cq] <= -REL_MAX_DISTANCE
            roff = pl.multiple_of(tt * sk, sk)
            coff = pl.multiple_of(cc * CHAIN_W, CHAIN_W)

            @pl.when(jnp.logical_or(blk_after, blk_before))
            def _():
                c = jnp.where(blk_after, b_after, b_before)
                bias_ref[slot, pl.ds(roff, sk), pl.ds(coff, CHAIN_W)] = jnp.full((sk, CHAIN_W), c, F32)

            @pl.when(jnp.logical_not(jnp.logical_or(blk_after, blk_before)))
            def _():
                pk = posk_ref[pl.ds(pl.multiple_of(off + roff, sk), sk), :]
                pk = jnp.concatenate([pk] * (CHAIN_W // LANE), axis=1)
                rel = pk - posq_ref[:, pl.ds(coff, CHAIN_W)]
                bias_ref[slot, pl.ds(roff, sk), pl.ds(coff, CHAIN_W)] = _t5_bias(rel, tab_ref, h)

            return 0

        lax.fori_loop(0, nsub * nc, fill, 0)

    def near_stage(i, slot, ms):
        ms, alpha = _softmax_stage(s_ref, p_ref, slot, nch, None, ms,
                                   bias=lambda j: bias_ref[slot, :, (j // 2) * CHAIN_W:(j // 2 + 1) * CHAIN_W])
        value_stage(i, slot, alpha)
        return ms

    def near_pair(g, ms):
        ia, ib = tile_at(nfar + 2 * g), tile_at(nfar + 2 * g + 1)
        fill_bias(ia, 0)
        fill_bias(ib, 1)
        score_stage(ia, 0)
        score_stage(ib, 1)
        ms = near_stage(ia, 0, ms)
        return near_stage(ib, 1, ms)

    def near_single(g, ms):
        i = tile_at(nk - 1)
        fill_bias(i, 0)
        score_stage(i, 0)
        return near_stage(i, 0, ms)

    ms = lax.fori_loop(0, (nk - nfar) // 2, near_pair, ms)
    lax.fori_loop(0, (nk - nfar) % 2, near_single, ms)

    lam = lam_ref[0]
    dv = DIFF_V_DIM
    for c in range(nc):
        o1 = acc_ref[2 * c, :dv] / acc_ref[2 * c, dv:dv + 1]
        o2 = acc_ref[2 * c + 1, :dv] / acc_ref[2 * c + 1, dv:dv + 1]
        o = o1 - lam * o2
        msq = jnp.mean(o * o, axis=0, keepdims=True)
        o = o * lax.rsqrt(msq + DIFF_SUBLN_EPS) * g_ref[...]
        o_ref[:, c * CHAIN_W:(c + 1) * CHAIN_W] = (o * out_scale).astype(o_ref.dtype)


def diff_attention(qT, qkv, vT, positions, rel_bias, lam, subln, lam_init, nc=4, tk=512, sk=128):
    s = qkv.shape[0]
    tq = _pick(s, nc * CHAIN_W, CHAIN_W)
    nc = tq // CHAIN_W
    tk = _pick(s, tk)
    sk = _pick(tk, sk)
    nk, nq = s // tk, s // tq
    pos = positions.reshape(s).astype(jnp.int32)
    kmin = pos.reshape(nk, tk).min(axis=1)
    kmax = pos.reshape(nk, tk).max(axis=1)
    smin = pos.reshape(s // sk, sk).min(axis=1)
    smax = pos.reshape(s // sk, sk).max(axis=1)
    qmin = pos.reshape(nq, tq).min(axis=1)
    qmax = pos.reshape(nq, tq).max(axis=1)
    cmin = pos.reshape(s // CHAIN_W, CHAIN_W).min(axis=1)
    cmax = pos.reshape(s // CHAIN_W, CHAIN_W).max(axis=1)
    far = jnp.logical_or(kmin[None, :] - qmax[:, None] >= REL_MAX_DISTANCE,
                         kmax[None, :] - qmin[:, None] <= -REL_MAX_DISTANCE)
    order = jnp.argsort(jnp.logical_not(far), axis=1, stable=True).astype(jnp.int32)
    nfar = far.sum(axis=1).astype(jnp.int32)
    nfar = nfar - nfar % 2
    posq = pos.reshape(1, s)
    posk = jnp.broadcast_to(pos.reshape(s, 1), (s, LANE))
    tab = (rel_bias.T * LOG2E).astype(F32)
    dva = DIFF_V_DIM + SUM_ROWS
    smem = pl.BlockSpec(memory_space=pltpu.SMEM)
    grid_spec = pltpu.PrefetchScalarGridSpec(
        num_scalar_prefetch=10,
        grid=(DIFF_HEADS, nq),
        in_specs=[smem, smem,
                  pl.BlockSpec((DIFF_V_DIM, tq), lambda h, i, *_: (h, i)),
                  pl.BlockSpec((s, 2 * DIFF_HEAD_DIM), lambda h, i, *_: (0, DIFF_HEADS + h)),
                  pl.BlockSpec((DIFF_V_DIM, s), lambda h, i, *_: (h, 0)),
                  pl.BlockSpec((1, tq), lambda h, i, *_: (0, i)),
                  pl.BlockSpec((s, LANE), lambda h, i, *_: (0, 0)),
                  pl.BlockSpec((DIFF_V_DIM, 1), lambda h, i, *_: (0, 0))],
        out_specs=pl.BlockSpec((DIFF_V_DIM, tq), lambda h, i, *_: (h, i)),
        scratch_shapes=[pltpu.VMEM((2, DIFF_V_DIM, tq), BF16),
                        pltpu.VMEM((2 * nc, dva, CHAIN_W), F32),
                        pltpu.VMEM((2, 2 * nc, tk, CHAIN_W), F32),
                        pltpu.VMEM((2, 2 * nc, tk, CHAIN_W), BF16),
                        pltpu.VMEM((2, tk, tq), F32)],
    )
    return pl.pallas_call(
        functools.partial(_diff_attn_body, tk=tk, nk=nk, nc=nc, sk=sk, out_scale=1.0 - lam_init),
        grid_spec=grid_spec,
        out_shape=jax.ShapeDtypeStruct((DIFF_HEADS * DIFF_V_DIM, s), BF16),
        compiler_params=_cparams(("parallel", "parallel")),
        name="diff_attention",
    )(order.reshape(nq * nk), nfar, kmin, kmax, smin, smax, qmin, qmax, cmin, cmax,
      tab, lam.reshape(1).astype(F32),
      qT, qkv, vT, posq, posk,
      subln.reshape(DIFF_V_DIM, 1).astype(F32))


def _xa_body(x_ref, gpre_ref, wq_ref, kT_ref, v_ref, wo_ref, gpost_ref, gnext_ref, o_ref, h_ref):
    x = x_ref[...]
    h = (_rms(x, NORM_EPS) * gpre_ref[...]).astype(BF16)
    q = jnp.dot(h, wq_ref[...], preferred_element_type=F32) * (MEM_HEAD_DIM ** -0.5)
    q = q.astype(BF16)
    outs = []
    for hd in range(MEM_HEADS):
        sl = slice(hd * MEM_HEAD_DIM, (hd + 1) * MEM_HEAD_DIM)
        s = jnp.dot(q[:, sl], kT_ref[sl, :], preferred_element_type=F32)
        p = jnp.exp(s - jnp.max(s, axis=-1, keepdims=True))
        l = jnp.sum(p, axis=-1, keepdims=True)
        o = jnp.dot(p.astype(BF16), v_ref[:, sl], preferred_element_type=F32) / l
        outs.append(o.astype(BF16))
    o = jnp.concatenate(outs, axis=1)
    y = jnp.dot(o, wo_ref[...], preferred_element_type=F32)
    x = x + _rms(y, NORM_EPS) * gpost_ref[...]
    o_ref[...] = x
    h_ref[...] = (_rms(x, NORM_EPS) * gnext_ref[...]).astype(h_ref.dtype)


def cross_attention_sublayer(x, g_pre, w_q, kT, v, w_o, l, g_post, g_next):
    s, d = x.shape
    ts = _pick(s, 256, 8)
    row = lambda i: (i, 0)
    fixed = lambda i: (0, 0)
    return pl.pallas_call(
        _xa_body,
        grid=(s // ts,),
        in_specs=[pl.BlockSpec((ts, d), row), pl.BlockSpec((1, d), fixed),
                  pl.BlockSpec((None,) + w_q.shape[1:], lambda i: (l, 0, 0)), pl.BlockSpec(kT.shape, fixed),
                  pl.BlockSpec(v.shape, fixed), pl.BlockSpec((None,) + w_o.shape[1:], lambda i: (l, 0, 0)),
                  pl.BlockSpec((1, d), fixed), pl.BlockSpec((1, d), fixed)],
        out_specs=[pl.BlockSpec((ts, d), row), pl.BlockSpec((ts, d), row)],
        out_shape=[jax.ShapeDtypeStruct((s, d), F32), jax.ShapeDtypeStruct((s, d), BF16)],
        compiler_params=_cparams(("parallel",)),
        name="cross_attention",
    )(x, g_pre.reshape(1, d), w_q, kT, v, w_o, g_post.reshape(1, d), g_next.reshape(1, d))


def _rot_cols(w):
    half = w.shape[-1] // 2
    return jnp.concatenate([-w[..., half:], w[..., :half]], axis=-1)


def _prep_w_in(w, q_rank, kv_rank):
    a = q_rank + kv_rank
    kr = w[..., a:a + MLA_ROPE_DIM]
    z = jnp.zeros_like(kr)
    w_lat = jnp.concatenate([w[..., :a], kr, z, _rot_cols(kr), z], axis=-1)
    b = a + MLA_ROPE_DIM
    nq = DIFF_HEADS * DIFF_V_DIM
    w_q = w[..., b:b + nq] * (LOG2E * DIFF_HEAD_DIM ** -0.5)
    w_qkvd = jnp.concatenate([w_q, w[..., b + nq:b + 3 * nq]], axis=-1)
    return w_lat.astype(BF16), w_qkvd.astype(BF16), w[..., b + 3 * nq:].astype(BF16)


def _prep_w_uq(w):
    nl, r, _ = w.shape
    w4 = w.reshape(nl, r, MLA_HEADS, MLA_QK_DIM)
    nope, rope = w4[..., :MLA_NOPE_DIM], w4[..., MLA_NOPE_DIM:]
    z = jnp.zeros_like(rope)
    out = jnp.concatenate([nope, rope, z, _rot_cols(rope), z], axis=-1)
    return out.reshape(nl, r, -1).astype(BF16)


def _prep_w_ukv(w):
    nl, r, _ = w.shape
    w4 = w.reshape(nl, r, MLA_HEADS, MLA_NOPE_DIM + MLA_V_DIM)
    k_nope = w4[..., :MLA_NOPE_DIM].reshape(nl, r, -1)
    v = w4[..., MLA_NOPE_DIM:].reshape(nl, r, -1)
    return jnp.concatenate([k_nope, v], axis=-1).astype(BF16)


def _rope_tables(positions, s):
    half = MLA_ROPE_DIM // 2
    inv_freq = ROPE_BASE ** (-jnp.arange(half, dtype=F32) / half)
    ang = positions.reshape(s, 1).astype(F32) * inv_freq
    z = jnp.zeros((s, LANE - MLA_ROPE_DIM), F32)
    cos = jnp.concatenate([jnp.cos(ang), jnp.cos(ang), z], axis=1)
    sin = jnp.concatenate([jnp.sin(ang), jnp.sin(ang), z], axis=1)
    return cos, sin


def kernel(x, mem, positions, rel_bias, mix_norm_pre, mix_norm_post, w_in, mla_q_norm, mla_w_uq, mla_kv_norm, mla_w_ukv, diff_lambda, diff_subln, w_mla_branch, w_diff_branch, w_out, xa_norm_pre, xa_norm_post, xa_mem_norm, xa_w_q, xa_w_kv, xa_w_o, ffn_norm_pre, ffn_norm_post, ffn_w_in, ffn_w_out):
    b, s, d = x.shape
    assert b == 1, "kernel is written for batch 1"
    depth = w_in.shape[0]
    q_rank, kv_rank = mla_q_norm.shape[1], mla_kv_norm.shape[1]
    dff = ffn_w_out.shape[1]
    dff_pad = -(-dff // 512) * 512
    nh = DIFF_HEADS * DIFF_V_DIM

    xs = x.reshape(s, d)
    mems = mem.reshape(mem.shape[1], d)
    cos, sin = _rope_tables(positions, s)

    w_lat, w_qkvd, w_gate = _prep_w_in(w_in, q_rank, kv_rank)
    w_uq, w_ukv = _prep_w_uq(mla_w_uq), _prep_w_ukv(mla_w_ukv)
    w_mb, w_db, w_o = w_mla_branch.astype(BF16), w_diff_branch.astype(BF16), w_out.astype(BF16)
    w_xq, w_xkv, w_xo = xa_w_q.astype(BF16), xa_w_kv.astype(BF16), xa_w_o.astype(BF16)
    pad = dff_pad - dff
    w_fi = ffn_w_in.astype(BF16)
    w_fg = jnp.pad(w_fi[..., :dff], ((0, 0), (0, 0), (0, pad)))
    w_fu = jnp.pad(w_fi[..., dff:], ((0, 0), (0, 0), (0, pad)))
    w_fo = jnp.pad(ffn_w_out.astype(BF16), ((0, 0), (0, pad), (0, 0)))

    h = rmsnorm(xs, mix_norm_pre[0], BF16)
    for l in range(depth):
        lat = matmul(h, w_lat, l, F32, bn=768)
        qkvd = matmul(h, w_qkvd, l, BF16)
        gates = matmul(h, w_gate, l, BF16)

        q, k, v = mla_prep(lat, mla_q_norm[l], mla_kv_norm[l], w_uq, w_ukv, l, cos, sin)
        o_mla = mla_attention(q.T, k, v.T).T

        lam_init = 0.8 - 0.6 * math.exp(-0.3 * l)
        lv = diff_lambda[l].astype(F32)
        lam = jnp.exp(jnp.sum(lv[0] * lv[1])) - jnp.exp(jnp.sum(lv[2] * lv[3])) + lam_init
        qT_d = qkvd[:, :nh].T
        vT_d = qkvd[:, 2 * nh:].T
        o_diff = diff_attention(qT_d, qkvd, vT_d, positions, rel_bias, lam, diff_subln[l], lam_init).T

        merged = gated_merge(o_mla, o_diff, w_mb, w_db, l, gates)
        y = matmul(merged, w_o, l, F32)
        xs = add_rmsnorm(xs, y, mix_norm_post[l])

        mem_n = rmsnorm(mems, xa_mem_norm[l], BF16)
        kv = matmul(mem_n, w_xkv, l, BF16)
        hm = MEM_HEADS * MEM_HEAD_DIM
        xs, h = cross_attention_sublayer(xs, xa_norm_pre[l], w_xq, kv[:, :hm].T, kv[:, hm:], w_xo, l,
                                         xa_norm_post[l], ffn_norm_pre[l])

        act = swiglu_matmul(h, w_fg, w_fu, l)
        y = matmul(act, w_fo, l, F32, bk=2816)
        if l + 1 < depth:
            xs, h = add_rmsnorm(xs, y, ffn_norm_post[l], mix_norm_pre[l + 1])
        else:
            xs = add_rmsnorm(xs, y, ffn_norm_post[l])

    return xs.reshape(b, s, d)
```

```python
import functools
import math

import jax
import jax.numpy as jnp
from jax import lax
from jax.experimental import pallas as pl
from jax.experimental.pallas import tpu as pltpu

F32 = jnp.float32
BF16 = jnp.bfloat16

MLA_HEADS = 8
MLA_NOPE_DIM = 128
MLA_ROPE_DIM = 64
MLA_V_DIM = 128
MLA_QK_DIM = MLA_NOPE_DIM + MLA_ROPE_DIM
DIFF_HEADS = 8
DIFF_HEAD_DIM = 64
DIFF_V_DIM = 2 * DIFF_HEAD_DIM
MEM_HEADS = 4
MEM_HEAD_DIM = 128
REL_BUCKETS = 32
REL_MAX_DISTANCE = 128
ROPE_BASE = 10000.0
NORM_EPS = 1e-6
DIFF_SUBLN_EPS = 1e-5

LANE = 128
VMEM_LIMIT_BYTES = 56 * 1024 * 1024
NEG_BIG = -1e30
LOG2E = math.log2(math.e)


def _cparams(sem):
    return pltpu.CompilerParams(dimension_semantics=sem, vmem_limit_bytes=VMEM_LIMIT_BYTES)


def _pick(dim, target, align=LANE):
    if dim <= target:
        return dim
    t = (target // align) * align
    while t >= align:
        if dim % t == 0:
            return t
        t -= align
    return dim


def _sigmoid(x):
    return 1.0 / (1.0 + jnp.exp(-x))


def _rms(x, eps):
    return x * lax.rsqrt(jnp.mean(x * x, axis=-1, keepdims=True) + eps)


def _rmsnorm_body(x_ref, g_ref, o_ref, *, eps):
    x = x_ref[...].astype(F32)
    o_ref[...] = (_rms(x, eps) * g_ref[...]).astype(o_ref.dtype)


def rmsnorm(x, g, out_dtype, eps=NORM_EPS):
    m, d = x.shape
    bm = _pick(m, 256, 8)
    return pl.pallas_call(
        functools.partial(_rmsnorm_body, eps=eps),
        grid=(m // bm,),
        in_specs=[pl.BlockSpec((bm, d), lambda i: (i, 0)), pl.BlockSpec((1, d), lambda i: (0, 0))],
        out_specs=pl.BlockSpec((bm, d), lambda i: (i, 0)),
        out_shape=jax.ShapeDtypeStruct((m, d), out_dtype),
        compiler_params=_cparams(("parallel",)),
        name="rmsnorm",
    )(x, g.reshape(1, d))


def _add_rmsnorm_body(x_ref, y_ref, g_ref, *rest, eps, with_next):
    y = y_ref[...].astype(F32)
    x = x_ref[...] + _rms(y, eps) * g_ref[...]
    if with_next:
        gn_ref, o_ref, h_ref = rest
        h_ref[...] = (_rms(x, eps) * gn_ref[...]).astype(h_ref.dtype)
    else:
        (o_ref,) = rest
    o_ref[...] = x


def add_rmsnorm(x, y, g, g_next=None, eps=NORM_EPS):
    m, d = x.shape
    bm = _pick(m, 256, 8)
    row = pl.BlockSpec((bm, d), lambda i: (i, 0))
    vec = pl.BlockSpec((1, d), lambda i: (0, 0))
    with_next = g_next is not None
    args = (x, y, g.reshape(1, d)) + ((g_next.reshape(1, d),) if with_next else ())
    return pl.pallas_call(
        functools.partial(_add_rmsnorm_body, eps=eps, with_next=with_next),
        grid=(m // bm,),
        in_specs=[row, row, vec] + ([vec] if with_next else []),
        out_specs=[row, row] if with_next else row,
        out_shape=([jax.ShapeDtypeStruct((m, d), F32), jax.ShapeDtypeStruct((m, d), BF16)] if with_next
                   else jax.ShapeDtypeStruct((m, d), F32)),
        compiler_params=_cparams(("parallel",)),
        name="add_rmsnorm",
    )(*args)


def _matmul_body(a_ref, w_ref, o_ref, *scratch, nk):
    part = jnp.dot(a_ref[...], w_ref[...], preferred_element_type=F32)
    if nk == 1:
        o_ref[...] = part.astype(o_ref.dtype)
        return
    (acc_ref,) = scratch
    k = pl.program_id(2)

    @pl.when(k == 0)
    def _():
        acc_ref[...] = part

    @pl.when(k > 0)
    def _():
        acc_ref[...] += part

    @pl.when(k == nk - 1)
    def _():
        o_ref[...] = acc_ref[...].astype(o_ref.dtype)


def matmul(a, w, l, out_dtype, bm=1024, bn=1024, bk=4096):
    m, kd = a.shape
    _, _, n = w.shape
    bm, bn, bk = _pick(m, bm, 8), _pick(n, bn), _pick(kd, bk)
    nk = kd // bk
    return pl.pallas_call(
        functools.partial(_matmul_body, nk=nk),
        grid=(m // bm, n // bn, nk),
        in_specs=[pl.BlockSpec((bm, bk), lambda i, j, k: (i, k)),
                  pl.BlockSpec((None, bk, bn), lambda i, j, k: (l, k, j))],
        out_specs=pl.BlockSpec((bm, bn), lambda i, j, k: (i, j)),
        out_shape=jax.ShapeDtypeStruct((m, n), out_dtype),
        scratch_shapes=[pltpu.VMEM((bm, bn), F32)] if nk > 1 else [],
        compiler_params=_cparams(("parallel", "parallel", "arbitrary")),
        name="matmul",
    )(a, w)


def _swiglu_body(a_ref, wg_ref, wu_ref, o_ref):
    a = a_ref[...]
    g = jnp.dot(a, wg_ref[...], preferred_element_type=F32)
    u = jnp.dot(a, wu_ref[...], preferred_element_type=F32)
    o_ref[...] = (g * _sigmoid(g) * u).astype(o_ref.dtype)


def swiglu_matmul(a, w_gu, l, dff, bm=1024, bn=512):
    m, kd = a.shape
    bm, bn = _pick(m, bm, 8), _pick(dff, bn)
    nj = dff // bn
    return pl.pallas_call(
        _swiglu_body,
        grid=(m // bm, nj),
        in_specs=[pl.BlockSpec((bm, kd), lambda i, j: (i, 0)),
                  pl.BlockSpec((None, kd, bn), lambda i, j: (l, 0, j)),
                  pl.BlockSpec((None, kd, bn), lambda i, j: (l, 0, j + nj))],
        out_specs=pl.BlockSpec((bm, bn), lambda i, j: (i, j)),
        out_shape=jax.ShapeDtypeStruct((m, dff), BF16),
        compiler_params=_cparams(("parallel", "parallel")),
        name="swiglu_matmul",
    )(a, w_gu, w_gu)


def _merge_body(om_ref, od_ref, wm_ref, wd_ref, gm_ref, gd_ref, o_ref):
    ym = jnp.dot(om_ref[...], wm_ref[...], preferred_element_type=F32)
    yd = jnp.dot(od_ref[...], wd_ref[...], preferred_element_type=F32)
    sm = _sigmoid(gm_ref[...].astype(F32))
    sd = _sigmoid(gd_ref[...].astype(F32))
    o_ref[...] = (sm * ym + sd * yd).astype(o_ref.dtype)


def gated_merge(o_mla, o_diff, w_mla, w_diff, l, gates, bm=1024, bn=1024):
    m, km = o_mla.shape
    _, kd = o_diff.shape
    _, _, n = w_mla.shape
    bm, bn = _pick(m, bm, 8), _pick(n, bn)
    nj = n // bn
    return pl.pallas_call(
        _merge_body,
        grid=(m // bm, nj),
        in_specs=[pl.BlockSpec((bm, km), lambda i, j: (i, 0)),
                  pl.BlockSpec((bm, kd), lambda i, j: (i, 0)),
                  pl.BlockSpec((None, km, bn), lambda i, j: (l, 0, j)),
                  pl.BlockSpec((None, kd, bn), lambda i, j: (l, 0, j)),
                  pl.BlockSpec((bm, bn), lambda i, j: (i, j)),
                  pl.BlockSpec((bm, bn), lambda i, j: (i, j + nj))],
        out_specs=pl.BlockSpec((bm, bn), lambda i, j: (i, j)),
        out_shape=jax.ShapeDtypeStruct((m, n), BF16),
        compiler_params=_cparams(("parallel", "parallel")),
        name="gated_merge",
    )(o_mla, o_diff, w_mla, w_diff, gates, gates)


def _mla_prep_body(lat_ref, gq_ref, gkv_ref, wq_ref, wkv_ref, cos_ref, sin_ref,
                   q_ref, k_ref, v_ref, *, q_rank, kv_rank, scale):
    cos = cos_ref[...]
    sin = sin_ref[...]
    cq = lat_ref[:, :q_rank].astype(F32)
    ckv = lat_ref[:, q_rank:q_rank + kv_rank].astype(F32)
    kr = lat_ref[:, q_rank + kv_rank:q_rank + kv_rank + LANE].astype(F32)
    kr_rot = lat_ref[:, q_rank + kv_rank + LANE:q_rank + kv_rank + 2 * LANE].astype(F32)
    k_rope = (kr * cos + kr_rot * sin).astype(BF16)

    cqn = (_rms(cq, NORM_EPS) * gq_ref[...]).astype(BF16)
    qa = jnp.dot(cqn, wq_ref[...], preferred_element_type=F32)
    ckvn = (_rms(ckv, NORM_EPS) * gkv_ref[...]).astype(BF16)
    kv = jnp.dot(ckvn, wkv_ref[...], preferred_element_type=F32)

    hw = MLA_NOPE_DIM + 2 * LANE
    for h in range(MLA_HEADS):
        base = h * hw
        nope = qa[:, base:base + MLA_NOPE_DIM]
        rp = qa[:, base + MLA_NOPE_DIM:base + MLA_NOPE_DIM + LANE]
        rr = qa[:, base + MLA_NOPE_DIM + LANE:base + hw]
        q_ref[:, h * 256:h * 256 + 128] = (nope * scale).astype(BF16)
        q_ref[:, h * 256 + 128:(h + 1) * 256] = ((rp * cos + rr * sin) * scale).astype(BF16)
        k_ref[:, h * 256:h * 256 + 128] = kv[:, h * 128:(h + 1) * 128].astype(BF16)
        k_ref[:, h * 256 + 128:(h + 1) * 256] = k_rope
    v_ref[...] = kv[:, MLA_HEADS * MLA_NOPE_DIM:].astype(BF16)


def mla_prep(lat, gq, gkv, wq, wkv, l, cos, sin):
    s, lw = lat.shape
    q_rank, kv_rank = gq.shape[0], gkv.shape[0]
    ts = _pick(s, 256, 8)
    hq = MLA_HEADS * 256
    hv = MLA_HEADS * MLA_V_DIM
    row = lambda i: (i, 0)
    fixed = lambda i: (0, 0)
    return pl.pallas_call(
        functools.partial(_mla_prep_body, q_rank=q_rank, kv_rank=kv_rank, scale=LOG2E * MLA_QK_DIM ** -0.5),
        grid=(s // ts,),
        in_specs=[pl.BlockSpec((ts, lw), row),
                  pl.BlockSpec((1, q_rank), fixed), pl.BlockSpec((1, kv_rank), fixed),
                  pl.BlockSpec((None,) + wq.shape[1:], lambda i: (l, 0, 0)),
                  pl.BlockSpec((None,) + wkv.shape[1:], lambda i: (l, 0, 0)),
                  pl.BlockSpec((ts, LANE), row), pl.BlockSpec((ts, LANE), row)],
        out_specs=[pl.BlockSpec((ts, hq), row), pl.BlockSpec((ts, hq), row), pl.BlockSpec((ts, hv), row)],
        out_shape=[jax.ShapeDtypeStruct((s, hq), BF16), jax.ShapeDtypeStruct((s, hq), BF16),
                   jax.ShapeDtypeStruct((s, hv), BF16)],
        compiler_params=_cparams(("parallel",)),
        name="mla_prep",
    )(lat, gq.reshape(1, -1), gkv.reshape(1, -1), wq, wkv, cos, sin)


CHAIN_W = 256
SUM_ROWS = 16


def _softmax_stage(s_ref, p_ref, slot, n, cs, ms, bias=None):
    m_new, alpha = [], []
    for j in range(n):
        s = s_ref[slot, j]
        if bias is not None:
            s = s + bias(j)
        t = jnp.max(s, axis=0, keepdims=True)
        if cs is None:
            shift = m = jnp.maximum(ms[j], t)
        else:
            shift = jnp.maximum(ms[j] - cs, t)
            m = shift + cs
        p_ref[slot, j] = jnp.exp2(s - shift).astype(BF16)
        alpha.append(jnp.exp2(ms[j] - m))
        m_new.append(m)
    return tuple(m_new), tuple(alpha)


def _value_stage(acc_ref, p_ref, slot, n, alpha, vt):
    vt = jnp.concatenate([vt, jnp.ones((SUM_ROWS, vt.shape[1]), vt.dtype)], axis=0)
    for j in range(n):
        acc_ref[j] = alpha[j] * acc_ref[j] + jnp.dot(vt, p_ref[slot, j], preferred_element_type=F32)


def _mla_attn_body(qT_ref, k_ref, vT_ref, o_ref, acc_ref, s_ref, p_ref, *, tk, nk, nc, group_n):
    acc_ref[...] = jnp.zeros_like(acc_ref)

    def score_stage(i, slot):
        kt = k_ref[pl.ds(pl.multiple_of(i * tk, tk), tk), :]
        for c in range(nc):
            s_ref[slot, c] = jnp.dot(kt, qT_ref[:, c * CHAIN_W:(c + 1) * CHAIN_W], preferred_element_type=F32)

    def value_stage(i, slot, alpha):
        vt = vT_ref[:, pl.ds(pl.multiple_of(i * tk, tk), tk)]
        _value_stage(acc_ref, p_ref, slot, nc, alpha, vt)

    score_stage(0, 0)

    def group(g, ms):
        for u in range(group_n):
            i = g * group_n + u
            slot = u % 2
            score_stage(jnp.minimum(i + 1, nk - 1), 1 - slot)
            ms, alpha = _softmax_stage(s_ref, p_ref, slot, nc, None, ms)
            value_stage(i, slot, alpha)
        return ms

    neg = jnp.full((1, CHAIN_W), NEG_BIG, F32)
    lax.fori_loop(0, nk // group_n, group, (neg,) * nc)
    for c in range(nc):
        o = acc_ref[c, :MLA_V_DIM] / acc_ref[c, MLA_V_DIM:MLA_V_DIM + 1]
        o_ref[:, c * CHAIN_W:(c + 1) * CHAIN_W] = o.astype(o_ref.dtype)


def mla_attention(qT, k, vT, nc=8, tk=1024):
    s = k.shape[0]
    tq = _pick(s, nc * CHAIN_W, CHAIN_W)
    nc = tq // CHAIN_W
    tk = _pick(s // 2, tk)
    nk = s // tk
    group_n = 2
    assert nk % group_n == 0
    dva = MLA_V_DIM + SUM_ROWS
    return pl.pallas_call(
        functools.partial(_mla_attn_body, tk=tk, nk=nk, nc=nc, group_n=group_n),
        grid=(MLA_HEADS, s // tq),
        in_specs=[pl.BlockSpec((256, tq), lambda h, i: (h, i)),
                  pl.BlockSpec((s, 256), lambda h, i: (0, h)),
                  pl.BlockSpec((MLA_V_DIM, s), lambda h, i: (h, 0))],
        out_specs=pl.BlockSpec((MLA_V_DIM, tq), lambda h, i: (h, i)),
        out_shape=jax.ShapeDtypeStruct((MLA_HEADS * MLA_V_DIM, s), BF16),
        scratch_shapes=[pltpu.VMEM((nc, dva, CHAIN_W), F32),
                        pltpu.VMEM((2, nc, tk, CHAIN_W), F32),
                        pltpu.VMEM((2, nc, tk, CHAIN_W), BF16)],
        compiler_params=_cparams(("parallel", "parallel")),
        name="mla_attention",
    )(qT, k, vT)


T5_BUCKET_EDGES = (0, 1, 2, 3, 4, 5, 6, 7, 8, 12, 16, 23, 32, 46, 64, 91)


def _t5_bias(rel, tab_ref, h):
    nb = REL_BUCKETS // 2
    pos = rel > 0
    n = jnp.abs(rel)
    v = jnp.where(pos, tab_ref[h, nb], tab_ref[h, 0])
    for b in range(1, nb):
        leaf = jnp.where(pos, tab_ref[h, nb + b], tab_ref[h, b])
        v = jnp.where(n >= T5_BUCKET_EDGES[b], leaf, v)
    return v


def _diff_attn_body(order_ref, nfar_ref, kmin_ref, kmax_ref, smin_ref, smax_ref, qmin_ref, qmax_ref,
                    cmin_ref, cmax_ref,
                    tab_ref, lam_ref, qT_ref, k_ref, vT_ref, posq_ref, posk_ref, g_ref,
                    o_ref, qm_ref, acc_ref, s_ref, p_ref, bias_ref, *, tk, nk, nc, sk, out_scale):
    h = pl.program_id(0)
    qi = pl.program_id(1)
    nch = 2 * nc
    qT = qT_ref[...]
    row = lax.broadcasted_iota(jnp.int32, qT.shape, 0)
    zero = jnp.zeros_like(qT)
    qm_ref[0] = jnp.where(row < DIFF_HEAD_DIM, qT, zero)
    qm_ref[1] = jnp.where(row >= DIFF_HEAD_DIM, qT, zero)
    b_after = tab_ref[h, REL_BUCKETS - 1]
    b_before = tab_ref[h, REL_BUCKETS // 2 - 1]
    q_lo = qmin_ref[qi]
    q_hi = qmax_ref[qi]
    acc_ref[...] = jnp.zeros_like(acc_ref)
    nsub = tk // sk
    nfar = nfar_ref[qi]

    def tile_at(t):
        return order_ref[qi * nk + t]

    def score_stage(i, slot):
        kt = k_ref[pl.ds(pl.multiple_of(i * tk, tk), tk), :]
        for j in range(nch):
            q = qm_ref[j % 2, :, (j // 2) * CHAIN_W:(j // 2 + 1) * CHAIN_W]
            s_ref[slot, j] = jnp.dot(kt, q, preferred_element_type=F32)

    def value_stage(i, slot, alpha):
        vt = vT_ref[:, pl.ds(pl.multiple_of(i * tk, tk), tk)]
        _value_stage(acc_ref, p_ref, slot, nch, alpha, vt)

    def far_bias(i):
        return jnp.where(kmin_ref[i] - q_hi >= REL_MAX_DISTANCE, b_after, b_before)

    @pl.when(nfar > 0)
    def _():
        score_stage(tile_at(0), 0)

    def far_tile_at(t):
        return tile_at(jnp.minimum(t, nfar - 1))

    def far_group(t0, n, ms):
        tiles = [far_tile_at(t0 + u) for u in range(n + 1)]
        for u in range(n):
            slot = u % 2
            score_stage(tiles[u + 1], 1 - slot)
            ms, alpha = _softmax_stage(s_ref, p_ref, slot, nch, far_bias(tiles[u]), ms)
            value_stage(tiles[u], slot, alpha)
        return ms

    neg = jnp.full((1, CHAIN_W), NEG_BIG, F32)
    ms = lax.fori_loop(0, nfar // 8, lambda g, ms: far_group(8 * g, 8, ms), (neg,) * nch)
    ms = lax.fori_loop(0, (nfar % 8) // 4, lambda g, ms: far_group(nfar // 8 * 8, 4, ms), ms)
    ms = lax.fori_loop(0, (nfar % 4) // 2, lambda g, ms: far_group(nfar - 2, 2, ms), ms)

    def fill_bias(i, slot):
        off = pl.multiple_of(i * tk, tk)

        def fill(n, _):
            tt, cc = n // nc, n % nc
            u = i * nsub + tt
            cq = qi * nc + cc
            blk_after = smin_ref[u] - cmax_ref[cq] >= REL_MAX_DISTANCE
            blk_before = smax_ref[u] - cmin_ref[cq] <= -REL_MAX_DISTANCE
            roff = pl.multiple_of(tt * sk, sk)
            coff = pl.multiple_of(cc * CHAIN_W, CHAIN_W)

            @pl.when(jnp.logical_or(blk_after, blk_before))
            def _():
                c = jnp.where(blk_after, b_after, b_before)
                bias_ref[slot, pl.ds(roff, sk), pl.ds(coff, CHAIN_W)] = jnp.full((sk, CHAIN_W), c, F32)

            @pl.when(jnp.logical_not(jnp.logical_or(blk_after, blk_before)))
            def _():
                pk = posk_ref[pl.ds(pl.multiple_of(off + roff, sk), sk), :]
                pk = jnp.concatenate([pk] * (CHAIN_W // LANE), axis=1)
                rel = pk - posq_ref[:, pl.ds(coff, CHAIN_W)]
                bias_ref[slot, pl.ds(roff, sk), pl.ds(coff, CHAIN_W)] = _t5_bias(rel, tab_ref, h)

            return 0

        lax.fori_loop(0, nsub * nc, fill, 0)

    def near_stage(i, slot, ms):
        ms, alpha = _softmax_stage(s_ref, p_ref, slot, nch, None, ms,
                                   bias=lambda j: bias_ref[slot, :, (j // 2) * CHAIN_W:(j // 2 + 1) * CHAIN_W])
        value_stage(i, slot, alpha)
        return ms

    def near_pair(g, ms):
        ia, ib = tile_at(nfar + 2 * g), tile_at(nfar + 2 * g + 1)
        fill_bias(ia, 0)
        fill_bias(ib, 1)
        score_stage(ia, 0)
        score_stage(ib, 1)
        ms = near_stage(ia, 0, ms)
        return near_stage(ib, 1, ms)

    def near_single(g, ms):
        i = tile_at(nk - 1)
        fill_bias(i, 0)
        score_stage(i, 0)
        return near_stage(i, 0, ms)

    ms = lax.fori_loop(0, (nk - nfar) // 2, near_pair, ms)
    lax.fori_loop(0, (nk - nfar) % 2, near_single, ms)

    lam = lam_ref[0]
    dv = DIFF_V_DIM
    for c in range(nc):
        o1 = acc_ref[2 * c, :dv] / acc_ref[2 * c, dv:dv + 1]
        o2 = acc_ref[2 * c + 1, :dv] / acc_ref[2 * c + 1, dv:dv + 1]
        o = o1 - lam * o2
        msq = jnp.mean(o * o, axis=0, keepdims=True)
        o = o * lax.rsqrt(msq + DIFF_SUBLN_EPS) * g_ref[...]
        o_ref[:, c * CHAIN_W:(c + 1) * CHAIN_W] = (o * out_scale).astype(o_ref.dtype)


def diff_attention(qT, qkv, vT, positions, rel_bias, lam, subln, lam_init, nc=4, tk=512, sk=128):
    s = qkv.shape[0]
    tq = _pick(s, nc * CHAIN_W, CHAIN_W)
    nc = tq // CHAIN_W
    tk = _pick(s, tk)
    sk = _pick(tk, sk)
    nk, nq = s // tk, s // tq
    pos = positions.reshape(s).astype(jnp.int32)
    kmin = pos.reshape(nk, tk).min(axis=1)
    kmax = pos.reshape(nk, tk).max(axis=1)
    smin = pos.reshape(s // sk, sk).min(axis=1)
    smax = pos.reshape(s // sk, sk).max(axis=1)
    qmin = pos.reshape(nq, tq).min(axis=1)
    qmax = pos.reshape(nq, tq).max(axis=1)
    cmin = pos.reshape(s // CHAIN_W, CHAIN_W).min(axis=1)
    cmax = pos.reshape(s // CHAIN_W, CHAIN_W).max(axis=1)
    far = jnp.logical_or(kmin[None, :] - qmax[:, None] >= REL_MAX_DISTANCE,
                         kmax[None, :] - qmin[:, None] <= -REL_MAX_DISTANCE)
    order = jnp.argsort(jnp.logical_not(far), axis=1, stable=True).astype(jnp.int32)
    nfar = far.sum(axis=1).astype(jnp.int32)
    nfar = nfar - nfar % 2
    posq = pos.reshape(1, s)
    posk = jnp.broadcast_to(pos.reshape(s, 1), (s, LANE))
    tab = (rel_bias.T * LOG2E).astype(F32)
    dva = DIFF_V_DIM + SUM_ROWS
    smem = pl.BlockSpec(memory_space=pltpu.SMEM)
    grid_spec = pltpu.PrefetchScalarGridSpec(
        num_scalar_prefetch=10,
        grid=(DIFF_HEADS, nq),
        in_specs=[smem, smem,
                  pl.BlockSpec((DIFF_V_DIM, tq), lambda h, i, *_: (h, i)),
                  pl.BlockSpec((s, 2 * DIFF_HEAD_DIM), lambda h, i, *_: (0, DIFF_HEADS + h)),
                  pl.BlockSpec((DIFF_V_DIM, s), lambda h, i, *_: (h, 0)),
                  pl.BlockSpec((1, tq), lambda h, i, *_: (0, i)),
                  pl.BlockSpec((s, LANE), lambda h, i, *_: (0, 0)),
                  pl.BlockSpec((DIFF_V_DIM, 1), lambda h, i, *_: (0, 0))],
        out_specs=pl.BlockSpec((DIFF_V_DIM, tq), lambda h, i, *_: (h, i)),
        scratch_shapes=[pltpu.VMEM((2, DIFF_V_DIM, tq), BF16),
                        pltpu.VMEM((2 * nc, dva, CHAIN_W), F32),
                        pltpu.VMEM((2, 2 * nc, tk, CHAIN_W), F32),
                        pltpu.VMEM((2, 2 * nc, tk, CHAIN_W), BF16),
                        pltpu.VMEM((2, tk, tq), F32)],
    )
    return pl.pallas_call(
        functools.partial(_diff_attn_body, tk=tk, nk=nk, nc=nc, sk=sk, out_scale=1.0 - lam_init),
        grid_spec=grid_spec,
        out_shape=jax.ShapeDtypeStruct((DIFF_HEADS * DIFF_V_DIM, s), BF16),
        compiler_params=_cparams(("parallel", "parallel")),
        name="diff_attention",
    )(order.reshape(nq * nk), nfar, kmin, kmax, smin, smax, qmin, qmax, cmin, cmax,
      tab, lam.reshape(1).astype(F32),
      qT, qkv, vT, posq, posk,
      subln.reshape(DIFF_V_DIM, 1).astype(F32))


def _xa_body(x_ref, gpre_ref, wq_ref, kT_ref, v_ref, wo_ref, gpost_ref, gnext_ref, o_ref, h_ref):
    x = x_ref[...]
    h = (_rms(x, NORM_EPS) * gpre_ref[...]).astype(BF16)
    q = jnp.dot(h, wq_ref[...], preferred_element_type=F32) * (MEM_HEAD_DIM ** -0.5)
    q = q.astype(BF16)
    outs = []
    for hd in range(MEM_HEADS):
        sl = slice(hd * MEM_HEAD_DIM, (hd + 1) * MEM_HEAD_DIM)
        s = jnp.dot(q[:, sl], kT_ref[sl, :], preferred_element_type=F32)
        p = jnp.exp(s - jnp.max(s, axis=-1, keepdims=True))
        l = jnp.sum(p, axis=-1, keepdims=True)
        o = jnp.dot(p.astype(BF16), v_ref[:, sl], preferred_element_type=F32) / l
        outs.append(o.astype(BF16))
    o = jnp.concatenate(outs, axis=1)
    y = jnp.dot(o, wo_ref[...], preferred_element_type=F32)
    x = x + _rms(y, NORM_EPS) * gpost_ref[...]
    o_ref[...] = x
    h_ref[...] = (_rms(x, NORM_EPS) * gnext_ref[...]).astype(h_ref.dtype)


def cross_attention_sublayer(x, g_pre, w_q, kT, v, w_o, l, g_post, g_next):
    s, d = x.shape
    ts = _pick(s, 256, 8)
    row = lambda i: (i, 0)
    fixed = lambda i: (0, 0)
    return pl.pallas_call(
        _xa_body,
        grid=(s // ts,),
        in_specs=[pl.BlockSpec((ts, d), row), pl.BlockSpec((1, d), fixed),
                  pl.BlockSpec((None,) + w_q.shape[1:], lambda i: (l, 0, 0)), pl.BlockSpec(kT.shape, fixed),
                  pl.BlockSpec(v.shape, fixed), pl.BlockSpec((None,) + w_o.shape[1:], lambda i: (l, 0, 0)),
                  pl.BlockSpec((1, d), fixed), pl.BlockSpec((1, d), fixed)],
        out_specs=[pl.BlockSpec((ts, d), row), pl.BlockSpec((ts, d), row)],
        out_shape=[jax.ShapeDtypeStruct((s, d), F32), jax.ShapeDtypeStruct((s, d), BF16)],
        compiler_params=_cparams(("parallel",)),
        name="cross_attention",
    )(x, g_pre.reshape(1, d), w_q, kT, v, w_o, g_post.reshape(1, d), g_next.reshape(1, d))


def _rot_cols(w):
    half = w.shape[-1] // 2
    return jnp.concatenate([-w[..., half:], w[..., :half]], axis=-1)


def _prep_w_in(w, q_rank, kv_rank):
    a = q_rank + kv_rank
    kr = w[..., a:a + MLA_ROPE_DIM]
    z = jnp.zeros_like(kr)
    w_lat = jnp.concatenate([w[..., :a], kr, z, _rot_cols(kr), z], axis=-1)
    b = a + MLA_ROPE_DIM
    nq = DIFF_HEADS * DIFF_V_DIM
    w_q = w[..., b:b + nq] * (LOG2E * DIFF_HEAD_DIM ** -0.5)
    w_qkvd = jnp.concatenate([w_q, w[..., b + nq:b + 3 * nq]], axis=-1)
    return w_lat.astype(BF16), w_qkvd.astype(BF16), w[..., b + 3 * nq:].astype(BF16)


def _prep_w_uq(w):
    nl, r, _ = w.shape
    w4 = w.reshape(nl, r, MLA_HEADS, MLA_QK_DIM)
    nope, rope = w4[..., :MLA_NOPE_DIM], w4[..., MLA_NOPE_DIM:]
    z = jnp.zeros_like(rope)
    out = jnp.concatenate([nope, rope, z, _rot_cols(rope), z], axis=-1)
    return out.reshape(nl, r, -1).astype(BF16)


def _prep_w_ukv(w):
    nl, r, _ = w.shape
    w4 = w.reshape(nl, r, MLA_HEADS, MLA_NOPE_DIM + MLA_V_DIM)
    k_nope = w4[..., :MLA_NOPE_DIM].reshape(nl, r, -1)
    v = w4[..., MLA_NOPE_DIM:].reshape(nl, r, -1)
    return jnp.concatenate([k_nope, v], axis=-1).astype(BF16)


def _rope_tables(positions, s):
    half = MLA_ROPE_DIM // 2
    inv_freq = ROPE_BASE ** (-jnp.arange(half, dtype=F32) / half)
    ang = positions.reshape(s, 1).astype(F32) * inv_freq
    z = jnp.zeros((s, LANE - MLA_ROPE_DIM), F32)
    cos = jnp.concatenate([jnp.cos(ang), jnp.cos(ang), z], axis=1)
    sin = jnp.concatenate([jnp.sin(ang), jnp.sin(ang), z], axis=1)
    return cos, sin


def kernel(x, mem, positions, rel_bias, mix_norm_pre, mix_norm_post, w_in, mla_q_norm, mla_w_uq, mla_kv_norm, mla_w_ukv, diff_lambda, diff_subln, w_mla_branch, w_diff_branch, w_out, xa_norm_pre, xa_norm_post, xa_mem_norm, xa_w_q, xa_w_kv, xa_w_o, ffn_norm_pre, ffn_norm_post, ffn_w_in, ffn_w_out):
    b, s, d = x.shape
    assert b == 1, "kernel is written for batch 1"
    depth = w_in.shape[0]
    q_rank, kv_rank = mla_q_norm.shape[1], mla_kv_norm.shape[1]
    dff = ffn_w_out.shape[1]
    dff_pad = -(-dff // 512) * 512
    nh = DIFF_HEADS * DIFF_V_DIM

    xs = x.reshape(s, d)
    mems = mem.reshape(mem.shape[1], d)
    cos, sin = _rope_tables(positions, s)

    w_lat, w_qkvd, w_gate = _prep_w_in(w_in, q_rank, kv_rank)
    w_uq, w_ukv = _prep_w_uq(mla_w_uq), _prep_w_ukv(mla_w_ukv)
    w_mb, w_db, w_o = w_mla_branch.astype(BF16), w_diff_branch.astype(BF16), w_out.astype(BF16)
    w_xq, w_xkv, w_xo = xa_w_q.astype(BF16), xa_w_kv.astype(BF16), xa_w_o.astype(BF16)
    pad = dff_pad - dff
    w_gu = jnp.concatenate([jnp.pad(ffn_w_in[..., :dff], ((0, 0), (0, 0), (0, pad))),
                            jnp.pad(ffn_w_in[..., dff:], ((0, 0), (0, 0), (0, pad)))], axis=-1).astype(BF16)
    w_fo = jnp.pad(ffn_w_out, ((0, 0), (0, pad), (0, 0))).astype(BF16)

    h = rmsnorm(xs, mix_norm_pre[0], BF16)
    for l in range(depth):
        lat = matmul(h, w_lat, l, F32, bn=768)
        qkvd = matmul(h, w_qkvd, l, BF16)
        gates = matmul(h, w_gate, l, BF16)

        q, k, v = mla_prep(lat, mla_q_norm[l], mla_kv_norm[l], w_uq, w_ukv, l, cos, sin)
        o_mla = mla_attention(q.T, k, v.T).T

        lam_init = 0.8 - 0.6 * math.exp(-0.3 * l)
        lv = diff_lambda[l].astype(F32)
        lam = jnp.exp(jnp.sum(lv[0] * lv[1])) - jnp.exp(jnp.sum(lv[2] * lv[3])) + lam_init
        qT_d = qkvd[:, :nh].T
        vT_d = qkvd[:, 2 * nh:].T
        o_diff = diff_attention(qT_d, qkvd, vT_d, positions, rel_bias, lam, diff_subln[l], lam_init).T

        merged = gated_merge(o_mla, o_diff, w_mb, w_db, l, gates)
        y = matmul(merged, w_o, l, F32)
        xs = add_rmsnorm(xs, y, mix_norm_post[l])

        mem_n = rmsnorm(mems, xa_mem_norm[l], BF16)
        kv = matmul(mem_n, w_xkv, l, BF16)
        hm = MEM_HEADS * MEM_HEAD_DIM
        xs, h = cross_attention_sublayer(xs, xa_norm_pre[l], w_xq, kv[:, :hm].T, kv[:, hm:], w_xo, l,
                                         xa_norm_post[l], ffn_norm_pre[l])

        act = swiglu_matmul(h, w_gu, l, dff_pad)
        y = matmul(act, w_fo, l, F32, bm=512, bn=512, bk=dff_pad)
        if l + 1 < depth:
            xs, h = add_rmsnorm(xs, y, ffn_norm_post[l], mix_norm_pre[l + 1])
        else:
            xs = add_rmsnorm(xs, y, ffn_norm_post[l])

    return xs.reshape(b, s, d)
```
